```python
import jax, jax.numpy as jnp
from jax import lax
import numpy as np

D_MODEL = 1024
BATCH = 8
SEQ = 2048
DEPTH = 4
DEC_BATCH = 128
DEC_SEQ = 4
PAST_LEN = 16384
PAGE_SIZE = 128

W_A = D_MODEL // 2
W_B = D_MODEL // 2
W_C = D_MODEL // 2
D_MIX = W_A + W_B + W_C
P_IN = 3 * W_A + 3 * W_B + 4 * W_C
CONV_A = 31
CONV_C = 3
CHUNK = 128
HEADS_B = 4
RMS_EPS = 1e-6
LN_EPS = 1e-5

kernel_name = "hybrid_conformer_gmlp_shortconv_decoder_step"


def rmsnorm(x, g):
    xf = x.astype(jnp.float32)
    y = xf * lax.rsqrt(jnp.mean(xf * xf, axis=-1, keepdims=True) + RMS_EPS)
    return (y * g.astype(jnp.float32)).astype(x.dtype)


def layernorm(x, g, b):
    xf = x.astype(jnp.float32)
    mu = jnp.mean(xf, axis=-1, keepdims=True)
    xc = xf - mu
    var = jnp.mean(xc * xc, axis=-1, keepdims=True)
    y = xc * lax.rsqrt(var + LN_EPS) * g.astype(jnp.float32) + b.astype(jnp.float32)
    return y.astype(x.dtype)


def causal_dwconv(x_ext, w):
    ch = x_ext.shape[-1]
    return lax.conv_general_dilated(
        x_ext, w[:, None, :].astype(x_ext.dtype), window_strides=(1,), padding='VALID',
        dimension_numbers=('NWC', 'WIO', 'NWC'), feature_group_count=ch)


def chunk_spatial(v, ws, bs):
    n, t, w = v.shape
    L = min(t, CHUNK)
    vh = v.reshape(n, t // L, L, HEADS_B, w // HEADS_B)
    wm = jnp.tril(ws[:, :L, :L]).astype(v.dtype)
    s = jnp.einsum('hts,bnshd->bnthd', wm, vh)
    s = s + bs[:, :L].T[None, None, :, :, None].astype(v.dtype)
    return s.reshape(n, t, w)


def mixer_layer(x, c, hist_a, hist_c, w_mod, b_mod, g_pre, g_post, w_in, w_out,
                a_conv_w, a_conv_b, a_ln_g, a_ln_b, b_ln_g, b_ln_b, b_ws, b_bs, c_conv_w):
    mod = jnp.dot(jax.nn.silu(c), w_mod) + b_mod
    shift, scale, gate = jnp.split(mod, 3, axis=-1)
    h = rmsnorm(x, g_pre) * (1 + scale[:, None, :]) + shift[:, None, :]
    proj = jnp.einsum('btd,dp->btp', h, w_in)
    splits = list(np.cumsum([W_A, W_A, W_A, W_B, W_B, W_B, W_C, W_C, W_C, W_C])[:-1])
    a_val, a_gate, a_z, b_u, b_v, b_z, c_b, c_c, c_x, c_z = jnp.split(proj, splits, axis=-1)

    a = a_val * jax.nn.sigmoid(a_gate)
    a_ext = jnp.concatenate([hist_a.astype(a.dtype), a], axis=1)
    a_conv = causal_dwconv(a_ext, a_conv_w) + a_conv_b
    y_a = jax.nn.silu(layernorm(a_conv, a_ln_g, a_ln_b)) * jax.nn.silu(a_z)

    v_n = layernorm(b_v, b_ln_g, b_ln_b)
    y_b = b_u * chunk_spatial(v_n, b_ws, b_bs) * jax.nn.silu(b_z)

    cx = c_c * c_x
    c_ext = jnp.concatenate([hist_c.astype(cx.dtype), cx], axis=1)
    y_c = c_b * causal_dwconv(c_ext, c_conv_w) * jax.nn.silu(c_z)

    mix = jnp.einsum('btm,md->btd', jnp.concatenate([y_a, y_b, y_c], axis=-1), w_out)
    out = x + gate[:, None, :] * rmsnorm(mix, g_post)
    return out, a_ext[:, -(CONV_A - 1):], c_ext[:, -(CONV_C - 1):], v_n


def setup_inputs(seed: int = 0) -> dict:
    key = jax.random.key(seed)
    ks = jax.random.split(key, 24)
    f = jnp.float32
    nrm = lambda k, s, sc: jax.random.normal(k, s, f) * sc
    return {
        "x_prompt": nrm(ks[0], (BATCH, SEQ, D_MODEL), 1.0),
        "x_sample": nrm(ks[1], (DEC_BATCH, DEC_SEQ, D_MODEL), 1.0),
        "c_prompt": nrm(ks[2], (BATCH, D_MODEL), 1.0),
        "c_sample": nrm(ks[3], (DEC_BATCH, D_MODEL), 1.0),
        "state_conv_a": nrm(ks[4], (DEPTH, DEC_BATCH, CONV_A - 1, W_A), 0.5),
        "state_conv_c": nrm(ks[5], (DEPTH, DEC_BATCH, CONV_C - 1, W_C), 0.5),
        "w_mod": nrm(ks[6], (DEPTH, D_MODEL, 3 * D_MODEL), 0.5 * D_MODEL ** -0.5),
        "b_mod": nrm(ks[7], (DEPTH, 3 * D_MODEL), 0.01),
        "g_pre": 1.0 + nrm(ks[8], (DEPTH, D_MODEL), 0.05),
        "g_post": 1.0 + nrm(ks[9], (DEPTH, D_MODEL), 0.05),
        "w_in": nrm(ks[10], (DEPTH, D_MODEL, P_IN), D_MODEL ** -0.5),
        "w_out": nrm(ks[11], (DEPTH, D_MIX, D_MODEL), D_MIX ** -0.5),
        "a_conv_w": nrm(ks[12], (DEPTH, CONV_A, W_A), CONV_A ** -0.5),
        "a_conv_b": nrm(ks[13], (DEPTH, W_A), 0.01),
        "a_ln_g": 1.0 + nrm(ks[14], (DEPTH, W_A), 0.05),
        "a_ln_b": nrm(ks[15], (DEPTH, W_A), 0.01),
        "b_ln_g": 1.0 + nrm(ks[16], (DEPTH, W_B), 0.05),
        "b_ln_b": nrm(ks[17], (DEPTH, W_B), 0.01),
        "b_ws": nrm(ks[18], (DEPTH, HEADS_B, CHUNK, CHUNK), 0.5 * CHUNK ** -0.5),
        "b_bs": 1.0 + nrm(ks[19], (DEPTH, HEADS_B, CHUNK), 0.1),
        "c_conv_w": nrm(ks[20], (DEPTH, CONV_C, W_C), CONV_C ** -0.5),
    }


def reference(x_prompt, x_sample, c_prompt, c_sample, state_conv_a, state_conv_c,
              w_mod, b_mod, g_pre, g_post, w_in, w_out, a_conv_w, a_conv_b,
              a_ln_g, a_ln_b, b_ln_g, b_ln_b, b_ws, b_bs, c_conv_w):
    xp, xs = x_prompt, x_sample
    n_p = xp.shape[0]
    ca_p, cc_p, ca_s, cc_s, v_s = [], [], [], [], []
    for l in range(DEPTH):
        lp = (w_mod[l], b_mod[l], g_pre[l], g_post[l], w_in[l], w_out[l],
              a_conv_w[l], a_conv_b[l], a_ln_g[l], a_ln_b[l], b_ln_g[l], b_ln_b[l],
              b_ws[l], b_bs[l], c_conv_w[l])
        zeros_a = jnp.zeros((n_p, CONV_A - 1, W_A), xp.dtype)
        zeros_c = jnp.zeros((n_p, CONV_C - 1, W_C), xp.dtype)
        xp, sa, sc, _ = mixer_layer(xp, c_prompt, zeros_a, zeros_c, *lp)
        ca_p.append(sa)
        cc_p.append(sc)
        xs, sa2, sc2, vn = mixer_layer(xs, c_sample, state_conv_a[l], state_conv_c[l], *lp)
        ca_s.append(sa2)
        cc_s.append(sc2)
        v_s.append(vn)
    return (xp, xs, jnp.stack(ca_p), jnp.stack(cc_p), jnp.stack(ca_s), jnp.stack(cc_s), jnp.stack(v_s))
```

```python
import functools

import jax
import jax.numpy as jnp
from jax import lax
from jax.experimental import pallas as pl
from jax.experimental.pallas import tpu as pltpu

D_MODEL = 1024
BATCH = 8
SEQ = 2048
DEPTH = 4
DEC_BATCH = 128
DEC_SEQ = 4
W_A = 512
W_B = 512
W_C = 512
D_MIX = W_A + W_B + W_C
P_IN = 3 * W_A + 3 * W_B + 4 * W_C
CONV_A = 31
CONV_C = 3
CHUNK = 128
HEADS_B = 4
HEAD_W = W_B // HEADS_B
RMS_EPS = 1e-6
LN_EPS = 1e-5

LANES = 128
SUBLANES = 8
N_LANE_CHUNKS = W_A // LANES

COL_A = 0
COL_B = 3 * W_A
COL_C = 3 * W_A + 3 * W_B

TILE_T = 256
HIST_A = 32
HIST_C = 8
CONV_ROWS = 64
N_ALL = DEC_BATCH + BATCH
SAMPLE_N = 64

VMEM_LIMIT = 56 * 1024 * 1024

F32 = jnp.float32
BF16 = jnp.bfloat16


def _silu(x):
    return x * jax.nn.sigmoid(x)


def _layernorm(x, g, b):
    mu = jnp.mean(x, axis=-1, keepdims=True)
    xc = x - mu
    var = jnp.mean(xc * xc, axis=-1, keepdims=True)
    return xc * lax.rsqrt(var + LN_EPS) * g + b


def _mod_kernel(c_ref, w_ref, b_ref, o_ref):
    c = c_ref[...]
    s = _silu(c).astype(BF16)
    o_ref[...] = jnp.dot(s, w_ref[...].astype(BF16), preferred_element_type=F32) + b_ref[...]


def _mod_call(c_all, w_mod, b_mod):
    tn = D_MODEL
    return pl.pallas_call(
        _mod_kernel,
        grid=(DEPTH, 3 * D_MODEL // tn),
        in_specs=[
            pl.BlockSpec((N_ALL, D_MODEL), lambda l, n: (0, 0)),
            pl.BlockSpec((None, D_MODEL, tn), lambda l, n: (l, 0, n)),
            pl.BlockSpec((None, 1, tn), lambda l, n: (l, 0, n)),
        ],
        out_specs=pl.BlockSpec((None, N_ALL, tn), lambda l, n: (l, 0, n)),
        out_shape=jax.ShapeDtypeStruct((DEPTH, N_ALL, 3 * D_MODEL), F32),
        compiler_params=pltpu.CompilerParams(
            dimension_semantics=("arbitrary", "arbitrary"), vmem_limit_bytes=VMEM_LIMIT),
        name="mod",
    )(c_all, w_mod, b_mod.reshape(DEPTH, 1, 3 * D_MODEL))


def _prompt_kernel(x_ref, mod_ref, gpre_ref, gpost_ref, win_ref, wout_ref,
                   acw_ref, acb_ref, alng_ref, alnb_ref, blng_ref, blnb_ref,
                   ws_ref, bsb_ref, ccw_ref,
                   y_ref, ca_ref, cc_ref,
                   aext_ref, cext_ref, mix_ref):
    T = TILE_T
    j = pl.program_id(1)

    @pl.when(j == 0)
    def _():
        aext_ref[:, 0:HIST_A, :] = jnp.zeros((N_LANE_CHUNKS, HIST_A, LANES), F32)
        cext_ref[:, 0:HIST_C, :] = jnp.zeros((N_LANE_CHUNKS, HIST_C, LANES), F32)

    x = x_ref[...]
    shift = mod_ref[0:1, :]
    scale = mod_ref[1:2, :]
    gate = mod_ref[2:3, :]
    r = lax.rsqrt(jnp.mean(x * x, axis=-1, keepdims=True) + RMS_EPS)
    h = (x * r) * (gpre_ref[...] * (1.0 + scale)) + shift
    hb = h.astype(BF16)

    pa = jnp.dot(hb, win_ref[:, COL_A:COL_A + 3 * W_A], preferred_element_type=F32)
    a = pa[:, 0:W_A] * jax.nn.sigmoid(pa[:, W_A:2 * W_A])
    a_z = pa[:, 2 * W_A:3 * W_A]
    for c in range(N_LANE_CHUNKS):
        aext_ref[c, HIST_A:HIST_A + T, :] = a[:, c * LANES:(c + 1) * LANES]
    off_a = HIST_A - (CONV_A - 1)
    cols = []
    for c in range(N_LANE_CHUNKS):
        lanes = slice(c * LANES, (c + 1) * LANES)
        rows = []
        for r0 in range(0, T, CONV_ROWS):
            acc = jnp.broadcast_to(acb_ref[:, lanes], (CONV_ROWS, LANES))
            for k in range(CONV_A):
                s0 = r0 + off_a + k
                acc = acc + aext_ref[c, s0:s0 + CONV_ROWS, :] * acw_ref[k:k + 1, lanes]
            rows.append(acc)
        cols.append(jnp.concatenate(rows, axis=0))
    a_conv = jnp.concatenate(cols, axis=1)
    y_a = _silu(_layernorm(a_conv, alng_ref[...], alnb_ref[...])) * _silu(a_z)
    mix_ref[:, 0:W_A] = y_a.astype(BF16)

    @pl.when(j == pl.num_programs(1) - 1)
    def _():
        for c in range(N_LANE_CHUNKS):
            ca_ref[:, c * LANES:(c + 1) * LANES] = aext_ref[c, T + off_a:T + HIST_A, :]

    for c in range(N_LANE_CHUNKS):
        aext_ref[c, 0:HIST_A, :] = aext_ref[c, T:T + HIST_A, :]

    pb = jnp.dot(hb, win_ref[:, COL_B:COL_B + 3 * W_B], preferred_element_type=F32)
    b_u = pb[:, 0:W_B]
    b_z = pb[:, 2 * W_B:3 * W_B]
    v_n = _layernorm(pb[:, W_B:2 * W_B], blng_ref[...], blnb_ref[...])
    vb = v_n.astype(BF16)
    row_i = lax.broadcasted_iota(jnp.int32, (CHUNK, CHUNK), 0)
    col_i = lax.broadcasted_iota(jnp.int32, (CHUNK, CHUNK), 1)
    wm = [jnp.where(row_i >= col_i, ws_ref[hd], 0.0).astype(BF16) for hd in range(HEADS_B)]
    s_rows = []
    for ch in range(T // CHUNK):
        s_cols = []
        for hd in range(HEADS_B):
            vblk = vb[ch * CHUNK:(ch + 1) * CHUNK, hd * HEAD_W:(hd + 1) * HEAD_W]
            s_cols.append(jnp.dot(wm[hd], vblk, preferred_element_type=F32))
        s_rows.append(jnp.concatenate(s_cols, axis=1) + bsb_ref[...])
    s_full = jnp.concatenate(s_rows, axis=0)
    y_b = b_u * s_full * _silu(b_z)
    mix_ref[:, W_A:W_A + W_B] = y_b.astype(BF16)

    pc = jnp.dot(hb, win_ref[:, COL_C:COL_C + 4 * W_C], preferred_element_type=F32)
    c_b = pc[:, 0:W_C]
    cx = pc[:, W_C:2 * W_C] * pc[:, 2 * W_C:3 * W_C]
    c_z = pc[:, 3 * W_C:4 * W_C]
    for c in range(N_LANE_CHUNKS):
        cext_ref[c, HIST_C:HIST_C + T, :] = cx[:, c * LANES:(c + 1) * LANES]
    off_c = HIST_C - (CONV_C - 1)
    cols = []
    for c in range(N_LANE_CHUNKS):
        lanes = slice(c * LANES, (c + 1) * LANES)
        acc = cext_ref[c, off_c:off_c + T, :] * ccw_ref[0:1, lanes]
        for k in range(1, CONV_C):
            acc = acc + cext_ref[c, off_c + k:off_c + k + T, :] * ccw_ref[k:k + 1, lanes]
        cols.append(acc)
    c_conv = jnp.concatenate(cols, axis=1)
    y_c = c_b * c_conv * _silu(c_z)
    mix_ref[:, W_A + W_B:D_MIX] = y_c.astype(BF16)

    @pl.when(j == pl.num_programs(1) - 1)
    def _():
        for c in range(N_LANE_CHUNKS):
            cc_ref[:, c * LANES:(c + 1) * LANES] = cext_ref[c, T + off_c:T + HIST_C, :]

    for c in range(N_LANE_CHUNKS):
        cext_ref[c, 0:HIST_C, :] = cext_ref[c, T:T + HIST_C, :]

    mix = jnp.dot(mix_ref[...], wout_ref[...], preferred_element_type=F32)
    r2 = lax.rsqrt(jnp.mean(mix * mix, axis=-1, keepdims=True) + RMS_EPS)
    y_ref[...] = x + (mix * r2) * (gate * gpost_ref[...])


def _prompt_layer(l, x, mod4, g_pre, g_post, w_in, w_out, a_conv_w, a_conv_b,
                  a_ln_g, a_ln_b, b_ln_g, b_ln_b, b_ws, bsb, c_conv_w):
    nt = SEQ // TILE_T
    row = lambda w: pl.BlockSpec((None, 1, w), lambda b, j: (l, 0, 0))
    return pl.pallas_call(
        _prompt_kernel,
        grid=(BATCH, nt),
        in_specs=[
            pl.BlockSpec((TILE_T, D_MODEL), lambda b, j: (b * nt + j, 0)),
            pl.BlockSpec((None, None, 3, D_MODEL), lambda b, j: (l, DEC_BATCH + b, 0, 0)),
            row(D_MODEL), row(D_MODEL),
            pl.BlockSpec((None, D_MODEL, P_IN), lambda b, j: (l, 0, 0)),
            pl.BlockSpec((None, D_MIX, D_MODEL), lambda b, j: (l, 0, 0)),
            pl.BlockSpec((None, CONV_A, W_A), lambda b, j: (l, 0, 0)),
            row(W_A), row(W_A), row(W_A), row(W_B), row(W_B),
            pl.BlockSpec((None, HEADS_B, CHUNK, CHUNK), lambda b, j: (l, 0, 0, 0)),
            pl.BlockSpec((None, CHUNK, W_B), lambda b, j: (l, 0, 0)),
            pl.BlockSpec((None, CONV_C, W_C), lambda b, j: (l, 0, 0)),
        ],
        out_specs=[
            pl.BlockSpec((TILE_T, D_MODEL), lambda b, j: (b * nt + j, 0)),
            pl.BlockSpec((None, CONV_A - 1, W_A), lambda b, j: (b, 0, 0)),
            pl.BlockSpec((None, CONV_C - 1, W_C), lambda b, j: (b, 0, 0)),
        ],
        out_shape=[
            jax.ShapeDtypeStruct((BATCH * SEQ, D_MODEL), F32),
            jax.ShapeDtypeStruct((BATCH, CONV_A - 1, W_A), F32),
            jax.ShapeDtypeStruct((BATCH, CONV_C - 1, W_C), F32),
        ],
        scratch_shapes=[
            pltpu.VMEM((N_LANE_CHUNKS, TILE_T + HIST_A, LANES), F32),
            pltpu.VMEM((N_LANE_CHUNKS, TILE_T + HIST_C, LANES), F32),
            pltpu.VMEM((TILE_T, D_MIX), BF16),
        ],
        compiler_params=pltpu.CompilerParams(
            dimension_semantics=("arbitrary", "arbitrary"), vmem_limit_bytes=VMEM_LIMIT),
        name=f"prompt_layer{l}",
    )(x, mod4, g_pre, g_post, w_in, w_out, a_conv_w, a_conv_b,
      a_ln_g, a_ln_b, b_ln_g, b_ln_b, b_ws, bsb, c_conv_w)


def _sample_kernel(x_ref, mod_ref, gpre_ref, gpost_ref, win_ref, wout_ref,
                   acw_ref, acb_ref, alng_ref, alnb_ref, blng_ref, blnb_ref,
                   wss_ref, bss_ref, ccw_ref, ha_ref, hc_ref,
                   y_ref, ca_ref, cc_ref, v_ref,
                   mix_ref):
    S, N = DEC_SEQ, SAMPLE_N
    l = pl.program_id(1)

    @pl.when(l == 0)
    def _():
        y_ref[...] = x_ref[...]

    shift = mod_ref[:, 0:D_MODEL]
    scale = mod_ref[:, D_MODEL:2 * D_MODEL]
    gate = mod_ref[:, 2 * D_MODEL:3 * D_MODEL]
    pre = gpre_ref[...] * (1.0 + scale)
    xs = [y_ref[t] for t in range(S)]
    hs = []
    for t in range(S):
        r = lax.rsqrt(jnp.mean(xs[t] * xs[t], axis=-1, keepdims=True) + RMS_EPS)
        hs.append(((xs[t] * r) * pre + shift).astype(BF16))
    hb = jnp.concatenate(hs, axis=0)

    pa = jnp.dot(hb, win_ref[:, COL_A:COL_A + 3 * W_A], preferred_element_type=F32)
    a = pa[:, 0:W_A] * jax.nn.sigmoid(pa[:, W_A:2 * W_A])
    a_z = pa[:, 2 * W_A:3 * W_A]
    n_hist = CONV_A - 1
    a_ext = lambda i: ha_ref[i] if i < n_hist else a[(i - n_hist) * N:(i - n_hist + 1) * N]
    for k in range(n_hist):
        ca_ref[k] = a_ext(k + S)
    for t in range(S):
        acc = jnp.broadcast_to(acb_ref[...], (N, W_A))
        for k in range(CONV_A):
            acc = acc + a_ext(t + k) * acw_ref[k:k + 1, :]
        ln = _layernorm(acc, alng_ref[...], alnb_ref[...])
        y_a = _silu(ln) * _silu(a_z[t * N:(t + 1) * N])
        mix_ref[t * N:(t + 1) * N, 0:W_A] = y_a.astype(BF16)

    pb = jnp.dot(hb, win_ref[:, COL_B:COL_B + 3 * W_B], preferred_element_type=F32)
    v_n = _layernorm(pb[:, W_B:2 * W_B], blng_ref[...], blnb_ref[...])
    vs = [v_n[t * N:(t + 1) * N] for t in range(S)]
    for t in range(S):
        v_ref[t] = vs[t]
        s = jnp.broadcast_to(bss_ref[t:t + 1, :], (N, W_B))
        for u in range(t + 1):
            s = s + vs[u] * wss_ref[t, u:u + 1, :]
        rows = slice(t * N, (t + 1) * N)
        y_b = pb[rows, 0:W_B] * s * _silu(pb[rows, 2 * W_B:3 * W_B])
        mix_ref[rows, W_A:W_A + W_B] = y_b.astype(BF16)

    pc = jnp.dot(hb, win_ref[:, COL_C:COL_C + 4 * W_C], preferred_element_type=F32)
    cx = pc[:, W_C:2 * W_C] * pc[:, 2 * W_C:3 * W_C]
    n_hc = CONV_C - 1
    c_ext = lambda i: hc_ref[i] if i < n_hc else cx[(i - n_hc) * N:(i - n_hc + 1) * N]
    for k in range(n_hc):
        cc_ref[k] = c_ext(k + S)
    for t in range(S):
        acc = c_ext(t) * ccw_ref[0:1, :]
        for k in range(1, CONV_C):
            acc = acc + c_ext(t + k) * ccw_ref[k:k + 1, :]
        rows = slice(t * N, (t + 1) * N)
        y_c = pc[rows, 0:W_C] * acc * _silu(pc[rows, 3 * W_C:4 * W_C])
        mix_ref[rows, W_A + W_B:D_MIX] = y_c.astype(BF16)

    mix = jnp.dot(mix_ref[...], wout_ref[...], preferred_element_type=F32)
    r2 = lax.rsqrt(jnp.mean(mix * mix, axis=-1, keepdims=True) + RMS_EPS)
    post = gate * gpost_ref[...]
    for t in range(S):
        rows = slice(t * N, (t + 1) * N)
        y_ref[t] = xs[t] + (mix[rows] * r2[rows]) * post


def _sample_layers(x_t, mod_all, g_pre, g_post, w_in, w_out, a_conv_w, a_conv_b,
                   a_ln_g, a_ln_b, b_ln_g, b_ln_b, wss, bss, c_conv_w, ha_t, hc_t):
    S, N = DEC_SEQ, SAMPLE_N
    nsplit = DEC_BATCH // N
    row = lambda w: pl.BlockSpec((None, 1, w), lambda n, l: (l, 0, 0))
    lay3 = lambda a, b: pl.BlockSpec((None, a, b), lambda n, l: (l, 0, 0))
    lay4 = lambda a, b, c: pl.BlockSpec((None, a, b, c), lambda n, l: (l, 0, 0, 0))
    seq4 = lambda a, c: pl.BlockSpec((None, a, N, c), lambda n, l: (l, 0, n, 0))
    return pl.pallas_call(
        _sample_kernel,
        grid=(nsplit, DEPTH),
        in_specs=[
            pl.BlockSpec((S, N, D_MODEL), lambda n, l: (0, n, 0)),
            pl.BlockSpec((None, N, 3 * D_MODEL), lambda n, l: (l, n, 0)),
            row(D_MODEL), row(D_MODEL),
            lay3(D_MODEL, P_IN), lay3(D_MIX, D_MODEL),
            lay3(CONV_A, W_A),
            row(W_A), row(W_A), row(W_A), row(W_B), row(W_B),
            lay4(S, S, W_B), lay3(S, W_B), lay3(CONV_C, W_C),
            seq4(CONV_A - 1, W_A), seq4(CONV_C - 1, W_C),
        ],
        out_specs=[
            pl.BlockSpec((S, N, D_MODEL), lambda n, l: (0, n, 0)),
            seq4(CONV_A - 1, W_A), seq4(CONV_C - 1, W_C), seq4(S, W_B),
        ],
        out_shape=[
            jax.ShapeDtypeStruct((S, DEC_BATCH, D_MODEL), F32),
            jax.ShapeDtypeStruct((DEPTH, CONV_A - 1, DEC_BATCH, W_A), F32),
            jax.ShapeDtypeStruct((DEPTH, CONV_C - 1, DEC_BATCH, W_C), F32),
            jax.ShapeDtypeStruct((DEPTH, S, DEC_BATCH, W_B), F32),
        ],
        scratch_shapes=[pltpu.VMEM((S * N, D_MIX), BF16)],
        compiler_params=pltpu.CompilerParams(
            dimension_semantics=("arbitrary", "arbitrary"), vmem_limit_bytes=VMEM_LIMIT),
        name="sample_layers",
    )(x_t, mod_all, g_pre, g_post, w_in, w_out, a_conv_w, a_conv_b,
      a_ln_g, a_ln_b, b_ln_g, b_ln_b, wss, bss, c_conv_w, ha_t, hc_t)


def kernel(x_prompt, x_sample, c_prompt, c_sample, state_conv_a, state_conv_c, w_mod, b_mod,
           g_pre, g_post, w_in, w_out, a_conv_w, a_conv_b, a_ln_g, a_ln_b, b_ln_g, b_ln_b,
           b_ws, b_bs, c_conv_w):
    S = DEC_SEQ
    w_in_b = w_in.astype(BF16)
    w_out_b = w_out.astype(BF16)
    r3 = lambda p: p.reshape(DEPTH, 1, p.shape[-1])
    g_pre3, g_post3 = r3(g_pre), r3(g_post)
    acb3, alng3, alnb3, blng3, blnb3 = r3(a_conv_b), r3(a_ln_g), r3(a_ln_b), r3(b_ln_g), r3(b_ln_b)

    c_all = jnp.concatenate([c_sample, c_prompt], axis=0)
    mod_all = _mod_call(c_all, w_mod, b_mod)
    mod4 = mod_all.reshape(DEPTH, N_ALL, 3, D_MODEL)

    bsb = jnp.repeat(jnp.swapaxes(b_bs, 1, 2), HEAD_W, axis=2)
    wss = jnp.repeat(jnp.transpose(b_ws[:, :, :S, :S], (0, 2, 3, 1)), HEAD_W, axis=3)
    bss = bsb[:, :S, :]

    xp = x_prompt.reshape(BATCH * SEQ, D_MODEL)
    ca_p, cc_p = [], []
    for l in range(DEPTH):
        xp, ca, cc = _prompt_layer(l, xp, mod4, g_pre3, g_post3, w_in_b, w_out_b, a_conv_w, acb3,
                                   alng3, alnb3, blng3, blnb3, b_ws, bsb, c_conv_w)
        ca_p.append(ca)
        cc_p.append(cc)

    x_t = jnp.swapaxes(x_sample, 0, 1)
    ha_t = jnp.swapaxes(state_conv_a, 1, 2)
    hc_t = jnp.swapaxes(state_conv_c, 1, 2)
    y_t, ca_t, cc_t, v_t = _sample_layers(
        x_t, mod_all, g_pre3, g_post3, w_in_b, w_out_b, a_conv_w, acb3,
        alng3, alnb3, blng3, blnb3, wss, bss, c_conv_w, ha_t, hc_t)

    return (xp.reshape(BATCH, SEQ, D_MODEL),
            jnp.swapaxes(y_t, 0, 1),
            jnp.stack(ca_p), jnp.stack(cc_p),
            jnp.swapaxes(ca_t, 1, 2), jnp.swapaxes(cc_t, 1, 2), jnp.swapaxes(v_t, 1, 2))
```

```python
import functools

import jax
import jax.numpy as jnp
from jax import lax
from jax.experimental import pallas as pl
from jax.experimental.pallas import tpu as pltpu

D_MODEL = 1024
BATCH = 8
SEQ = 2048
DEPTH = 4
DEC_BATCH = 128
DEC_SEQ = 4
W_A = 512
W_B = 512
W_C = 512
D_MIX = W_A + W_B + W_C
P_IN = 3 * W_A + 3 * W_B + 4 * W_C
CONV_A = 31
CONV_C = 3
CHUNK = 128
HEADS_B = 4
HEAD_W = W_B // HEADS_B
RMS_EPS = 1e-6
LN_EPS = 1e-5

LANES = 128
SUBLANES = 8
N_LANE_CHUNKS = W_A // LANES

COL_A = 0
COL_B = 3 * W_A
COL_C = 3 * W_A + 3 * W_B

TILE_T = 256
HIST_A = 32
HIST_C = 8
CONV_ROWS = 64
N_ALL = DEC_BATCH + BATCH
SAMPLE_N = 64

VMEM_LIMIT = 56 * 1024 * 1024

F32 = jnp.float32
BF16 = jnp.bfloat16


def _silu(x):
    return x * jax.nn.sigmoid(x)


def _pack_rows(w):
    *lead, k, n = w.shape
    pairs = jnp.swapaxes(w.reshape(*lead, k // 2, 2, n), -1, -2)
    return lax.bitcast_convert_type(pairs, jnp.int32)


def _unpack_rows(w_packed):
    return pltpu.bitcast(w_packed, BF16)


def _layernorm(x, g, b):
    mu = jnp.mean(x, axis=-1, keepdims=True)
    xc = x - mu
    var = jnp.mean(xc * xc, axis=-1, keepdims=True)
    return xc * lax.rsqrt(var + LN_EPS) * g + b


def _mod_kernel(c_ref, w_ref, b_ref, o_ref):
    c = c_ref[...]
    s = _silu(c).astype(BF16)
    o_ref[...] = jnp.dot(s, w_ref[...].astype(BF16), preferred_element_type=F32) + b_ref[...]


def _mod_call(c_all, w_mod, b_mod):
    tn = D_MODEL
    return pl.pallas_call(
        _mod_kernel,
        grid=(DEPTH, 3 * D_MODEL // tn),
        in_specs=[
            pl.BlockSpec((N_ALL, D_MODEL), lambda l, n: (0, 0)),
            pl.BlockSpec((None, D_MODEL, tn), lambda l, n: (l, 0, n)),
            pl.BlockSpec((None, 1, tn), lambda l, n: (l, 0, n)),
        ],
        out_specs=pl.BlockSpec((None, N_ALL, tn), lambda l, n: (l, 0, n)),
        out_shape=jax.ShapeDtypeStruct((DEPTH, N_ALL, 3 * D_MODEL), F32),
        compiler_params=pltpu.CompilerParams(
            dimension_semantics=("arbitrary", "arbitrary"), vmem_limit_bytes=VMEM_LIMIT),
        name="mod",
    )(c_all, w_mod, b_mod.reshape(DEPTH, 1, 3 * D_MODEL))


def _prompt_kernel(x_ref, mod_ref, gpre_ref, gpost_ref, win_ref, wout_ref,
                   acw_ref, acb_ref, alng_ref, alnb_ref, blng_ref, blnb_ref,
                   ws_ref, bsb_ref, ccw_ref,
                   y_ref, ca_ref, cc_ref,
                   aext_ref, cext_ref, mix_ref):
    T = TILE_T
    j = pl.program_id(1)
    off_a = HIST_A - (CONV_A - 1)
    off_c = HIST_C - (CONV_C - 1)

    @pl.when(j == 0)
    def _():
        aext_ref[:, 0:HIST_A, :] = jnp.zeros((N_LANE_CHUNKS, HIST_A, LANES), F32)
        cext_ref[:, 0:HIST_C, :] = jnp.zeros((N_LANE_CHUNKS, HIST_C, LANES), F32)

    x = x_ref[...]
    shift = mod_ref[0:1, :]
    scale = mod_ref[1:2, :]
    gate = mod_ref[2:3, :]
    r = lax.rsqrt(jnp.mean(x * x, axis=-1, keepdims=True) + RMS_EPS)
    h = (x * r) * (gpre_ref[...] * (1.0 + scale)) + shift
    hb = h.astype(BF16)

    def proj(col):
        return jnp.dot(hb, _unpack_rows(win_ref[:, col:col + W_A]), preferred_element_type=F32)

    def conv_a_chunk(c):
        lanes = slice(c * LANES, (c + 1) * LANES)
        rows = []
        for r0 in range(0, T, CONV_ROWS):
            acc = jnp.broadcast_to(acb_ref[:, lanes], (CONV_ROWS, LANES))
            for k in range(CONV_A):
                s0 = r0 + off_a + k
                acc = acc + aext_ref[c, s0:s0 + CONV_ROWS, :] * acw_ref[k:k + 1, lanes]
            rows.append(acc)
        return jnp.concatenate(rows, axis=0)

    a = proj(COL_A) * jax.nn.sigmoid(proj(COL_A + W_A))
    for c in range(N_LANE_CHUNKS):
        aext_ref[c, HIST_A:HIST_A + T, :] = a[:, c * LANES:(c + 1) * LANES]
    b_v = proj(COL_B + W_B)
    cols = [conv_a_chunk(0)]
    a_z = proj(COL_A + 2 * W_A)
    cols.append(conv_a_chunk(1))
    b_u = proj(COL_B)
    cols.append(conv_a_chunk(2))
    b_z = proj(COL_B + 2 * W_B)
    cols.append(conv_a_chunk(3))
    c_c = proj(COL_C + W_C)
    a_conv = jnp.concatenate(cols, axis=1)
    y_a = _silu(_layernorm(a_conv, alng_ref[...], alnb_ref[...])) * _silu(a_z)
    mix_ref[:, 0:W_A] = y_a.astype(BF16)
    c_x = proj(COL_C + 2 * W_C)

    v_n = _layernorm(b_v, blng_ref[...], blnb_ref[...])
    vb = v_n.astype(BF16)
    c_b = proj(COL_C)
    row_i = lax.broadcasted_iota(jnp.int32, (CHUNK, CHUNK), 0)
    col_i = lax.broadcasted_iota(jnp.int32, (CHUNK, CHUNK), 1)
    wm = [jnp.where(row_i >= col_i, ws_ref[hd], 0.0).astype(BF16) for hd in range(HEADS_B)]
    s_rows = []
    for ch in range(T // CHUNK):
        s_cols = []
        for hd in range(HEADS_B):
            vblk = vb[ch * CHUNK:(ch + 1) * CHUNK, hd * HEAD_W:(hd + 1) * HEAD_W]
            s_cols.append(jnp.dot(wm[hd], vblk, preferred_element_type=F32))
        s_rows.append(jnp.concatenate(s_cols, axis=1) + bsb_ref[...])
    s_full = jnp.concatenate(s_rows, axis=0)

    cx = c_c * c_x
    for c in range(N_LANE_CHUNKS):
        cext_ref[c, HIST_C:HIST_C + T, :] = cx[:, c * LANES:(c + 1) * LANES]
    c_z = proj(COL_C + 3 * W_C)
    y_b = b_u * s_full * _silu(b_z)
    mix_ref[:, W_A:W_A + W_B] = y_b.astype(BF16)
    cols = []
    for c in range(N_LANE_CHUNKS):
        lanes = slice(c * LANES, (c + 1) * LANES)
        acc = cext_ref[c, off_c:off_c + T, :] * ccw_ref[0:1, lanes]
        for k in range(1, CONV_C):
            acc = acc + cext_ref[c, off_c + k:off_c + k + T, :] * ccw_ref[k:k + 1, lanes]
        cols.append(acc)
    c_conv = jnp.concatenate(cols, axis=1)
    y_c = c_b * c_conv * _silu(c_z)
    mix_ref[:, W_A + W_B:D_MIX] = y_c.astype(BF16)

    mix = jnp.dot(mix_ref[...], _unpack_rows(wout_ref[...]), preferred_element_type=F32)
    r2 = lax.rsqrt(jnp.mean(mix * mix, axis=-1, keepdims=True) + RMS_EPS)
    y_ref[...] = x + (mix * r2) * (gate * gpost_ref[...])

    @pl.when(j == pl.num_programs(1) - 1)
    def _():
        for c in range(N_LANE_CHUNKS):
            ca_ref[:, c * LANES:(c + 1) * LANES] = aext_ref[c, T + off_a:T + HIST_A, :]
            cc_ref[:, c * LANES:(c + 1) * LANES] = cext_ref[c, T + off_c:T + HIST_C, :]

    for c in range(N_LANE_CHUNKS):
        aext_ref[c, 0:HIST_A, :] = aext_ref[c, T:T + HIST_A, :]
        cext_ref[c, 0:HIST_C, :] = cext_ref[c, T:T + HIST_C, :]


def _prompt_layer(l, x, mod4, g_pre, g_post, w_in, w_out, a_conv_w, a_conv_b,
                  a_ln_g, a_ln_b, b_ln_g, b_ln_b, b_ws, bsb, c_conv_w):
    nt = SEQ // TILE_T
    row = lambda w: pl.BlockSpec((None, 1, w), lambda b, j: (l, 0, 0))
    return pl.pallas_call(
        _prompt_kernel,
        grid=(BATCH, nt),
        in_specs=[
            pl.BlockSpec((TILE_T, D_MODEL), lambda b, j: (b * nt + j, 0)),
            pl.BlockSpec((None, None, 3, D_MODEL), lambda b, j: (l, DEC_BATCH + b, 0, 0)),
            row(D_MODEL), row(D_MODEL),
            pl.BlockSpec((None, D_MODEL // 2, P_IN), lambda b, j: (l, 0, 0)),
            pl.BlockSpec((None, D_MIX // 2, D_MODEL), lambda b, j: (l, 0, 0)),
            pl.BlockSpec((None, CONV_A, W_A), lambda b, j: (l, 0, 0)),
            row(W_A), row(W_A), row(W_A), row(W_B), row(W_B),
            pl.BlockSpec((None, HEADS_B, CHUNK, CHUNK), lambda b, j: (l, 0, 0, 0)),
            pl.BlockSpec((None, CHUNK, W_B), lambda b, j: (l, 0, 0)),
            pl.BlockSpec((None, CONV_C, W_C), lambda b, j: (l, 0, 0)),
        ],
        out_specs=[
            pl.BlockSpec((TILE_T, D_MODEL), lambda b, j: (b * nt + j, 0)),
            pl.BlockSpec((None, CONV_A - 1, W_A), lambda b, j: (b, 0, 0)),
            pl.BlockSpec((None, CONV_C - 1, W_C), lambda b, j: (b, 0, 0)),
        ],
        out_shape=[
            jax.ShapeDtypeStruct((BATCH * SEQ, D_MODEL), F32),
            jax.ShapeDtypeStruct((BATCH, CONV_A - 1, W_A), F32),
            jax.ShapeDtypeStruct((BATCH, CONV_C - 1, W_C), F32),
        ],
        scratch_shapes=[
            pltpu.VMEM((N_LANE_CHUNKS, TILE_T + HIST_A, LANES), F32),
            pltpu.VMEM((N_LANE_CHUNKS, TILE_T + HIST_C, LANES), F32),
            pltpu.VMEM((TILE_T, D_MIX), BF16),
        ],
        compiler_params=pltpu.CompilerParams(
            dimension_semantics=("arbitrary", "arbitrary"), vmem_limit_bytes=VMEM_LIMIT),
        name=f"prompt_layer{l}",
    )(x, mod4, g_pre, g_post, w_in, w_out, a_conv_w, a_conv_b,
      a_ln_g, a_ln_b, b_ln_g, b_ln_b, b_ws, bsb, c_conv_w)


def _sample_kernel(x_ref, mod_ref, gpre_ref, gpost_ref, win_ref, wout_ref,
                   acw_ref, acb_ref, alng_ref, alnb_ref, blng_ref, blnb_ref,
                   wss_ref, bss_ref, ccw_ref, ha_ref, hc_ref,
                   y_ref, ca_ref, cc_ref, v_ref,
                   mix_ref):
    S, N = DEC_SEQ, SAMPLE_N
    l = pl.program_id(1)

    @pl.when(l == 0)
    def _():
        y_ref[...] = x_ref[...]

    shift = mod_ref[:, 0:D_MODEL]
    scale = mod_ref[:, D_MODEL:2 * D_MODEL]
    gate = mod_ref[:, 2 * D_MODEL:3 * D_MODEL]
    pre = gpre_ref[...] * (1.0 + scale)
    xs = [y_ref[t] for t in range(S)]
    hs = []
    for t in range(S):
        r = lax.rsqrt(jnp.mean(xs[t] * xs[t], axis=-1, keepdims=True) + RMS_EPS)
        hs.append(((xs[t] * r) * pre + shift).astype(BF16))
    hb = jnp.concatenate(hs, axis=0)

    pa = jnp.dot(hb, _unpack_rows(win_ref[:, COL_A:COL_A + 3 * W_A]), preferred_element_type=F32)
    a = pa[:, 0:W_A] * jax.nn.sigmoid(pa[:, W_A:2 * W_A])
    a_z = pa[:, 2 * W_A:3 * W_A]
    n_hist = CONV_A - 1
    a_ext = lambda i: ha_ref[i] if i < n_hist else a[(i - n_hist) * N:(i - n_hist + 1) * N]
    for k in range(n_hist):
        ca_ref[k] = a_ext(k + S)
    for t in range(S):
        acc = jnp.broadcast_to(acb_ref[...], (N, W_A))
        for k in range(CONV_A):
            acc = acc + a_ext(t + k) * acw_ref[k:k + 1, :]
        ln = _layernorm(acc, alng_ref[...], alnb_ref[...])
        y_a = _silu(ln) * _silu(a_z[t * N:(t + 1) * N])
        mix_ref[t * N:(t + 1) * N, 0:W_A] = y_a.astype(BF16)

    pb = jnp.dot(hb, _unpack_rows(win_ref[:, COL_B:COL_B + 3 * W_B]), preferred_element_type=F32)
    v_n = _layernorm(pb[:, W_B:2 * W_B], blng_ref[...], blnb_ref[...])
    vs = [v_n[t * N:(t + 1) * N] for t in range(S)]
    for t in range(S):
        v_ref[t] = vs[t]
        s = jnp.broadcast_to(bss_ref[t:t + 1, :], (N, W_B))
        for u in range(t + 1):
            s = s + vs[u] * wss_ref[t, u:u + 1, :]
        rows = slice(t * N, (t + 1) * N)
        y_b = pb[rows, 0:W_B] * s * _silu(pb[rows, 2 * W_B:3 * W_B])
        mix_ref[rows, W_A:W_A + W_B] = y_b.astype(BF16)

    pc = jnp.dot(hb, _unpack_rows(win_ref[:, COL_C:COL_C + 4 * W_C]), preferred_element_type=F32)
    cx = pc[:, W_C:2 * W_C] * pc[:, 2 * W_C:3 * W_C]
    n_hc = CONV_C - 1
    c_ext = lambda i: hc_ref[i] if i < n_hc else cx[(i - n_hc) * N:(i - n_hc + 1) * N]
    for k in range(n_hc):
        cc_ref[k] = c_ext(k + S)
    for t in range(S):
        acc = c_ext(t) * ccw_ref[0:1, :]
        for k in range(1, CONV_C):
            acc = acc + c_ext(t + k) * ccw_ref[k:k + 1, :]
        rows = slice(t * N, (t + 1) * N)
        y_c = pc[rows, 0:W_C] * acc * _silu(pc[rows, 3 * W_C:4 * W_C])
        mix_ref[rows, W_A + W_B:D_MIX] = y_c.astype(BF16)

    mix = jnp.dot(mix_ref[...], _unpack_rows(wout_ref[...]), preferred_element_type=F32)
    r2 = lax.rsqrt(jnp.mean(mix * mix, axis=-1, keepdims=True) + RMS_EPS)
    post = gate * gpost_ref[...]
    for t in range(S):
        rows = slice(t * N, (t + 1) * N)
        y_ref[t] = xs[t] + (mix[rows] * r2[rows]) * post


def _sample_layers(x_t, mod_all, g_pre, g_post, w_in, w_out, a_conv_w, a_conv_b,
                   a_ln_g, a_ln_b, b_ln_g, b_ln_b, wss, bss, c_conv_w, ha_t, hc_t):
    S, N = DEC_SEQ, SAMPLE_N
    nsplit = DEC_BATCH // N
    row = lambda w: pl.BlockSpec((None, 1, w), lambda n, l: (l, 0, 0))
    lay3 = lambda a, b: pl.BlockSpec((None, a, b), lambda n, l: (l, 0, 0))
    lay4 = lambda a, b, c: pl.BlockSpec((None, a, b, c), lambda n, l: (l, 0, 0, 0))
    seq4 = lambda a, c: pl.BlockSpec((None, a, N, c), lambda n, l: (l, 0, n, 0))
    return pl.pallas_call(
        _sample_kernel,
        grid=(nsplit, DEPTH),
        in_specs=[
            pl.BlockSpec((S, N, D_MODEL), lambda n, l: (0, n, 0)),
            pl.BlockSpec((None, N, 3 * D_MODEL), lambda n, l: (l, n, 0)),
            row(D_MODEL), row(D_MODEL),
            lay3(D_MODEL // 2, P_IN), lay3(D_MIX // 2, D_MODEL),
            lay3(CONV_A, W_A),
            row(W_A), row(W_A), row(W_A), row(W_B), row(W_B),
            lay4(S, S, W_B), lay3(S, W_B), lay3(CONV_C, W_C),
            seq4(CONV_A - 1, W_A), seq4(CONV_C - 1, W_C),
        ],
        out_specs=[
            pl.BlockSpec((S, N, D_MODEL), lambda n, l: (0, n, 0)),
            seq4(CONV_A - 1, W_A), seq4(CONV_C - 1, W_C), seq4(S, W_B),
        ],
        out_shape=[
            jax.ShapeDtypeStruct((S, DEC_BATCH, D_MODEL), F32),
            jax.ShapeDtypeStruct((DEPTH, CONV_A - 1, DEC_BATCH, W_A), F32),
            jax.ShapeDtypeStruct((DEPTH, CONV_C - 1, DEC_BATCH, W_C), F32),
            jax.ShapeDtypeStruct((DEPTH, S, DEC_BATCH, W_B), F32),
        ],
        scratch_shapes=[pltpu.VMEM((S * N, D_MIX), BF16)],
        compiler_params=pltpu.CompilerParams(
            dimension_semantics=("arbitrary", "arbitrary"), vmem_limit_bytes=VMEM_LIMIT),
        name="sample_layers",
    )(x_t, mod_all, g_pre, g_post, w_in, w_out, a_conv_w, a_conv_b,
      a_ln_g, a_ln_b, b_ln_g, b_ln_b, wss, bss, c_conv_w, ha_t, hc_t)


def kernel(x_prompt, x_sample, c_prompt, c_sample, state_conv_a, state_conv_c, w_mod, b_mod,
           g_pre, g_post, w_in, w_out, a_conv_w, a_conv_b, a_ln_g, a_ln_b, b_ln_g, b_ln_b,
           b_ws, b_bs, c_conv_w):
    S = DEC_SEQ
    w_in_b = _pack_rows(w_in.astype(BF16))
    w_out_b = _pack_rows(w_out.astype(BF16))
    r3 = lambda p: p.reshape(DEPTH, 1, p.shape[-1])
    g_pre3, g_post3 = r3(g_pre), r3(g_post)
    acb3, alng3, alnb3, blng3, blnb3 = r3(a_conv_b), r3(a_ln_g), r3(a_ln_b), r3(b_ln_g), r3(b_ln_b)

    c_all = jnp.concatenate([c_sample, c_prompt], axis=0)
    mod_all = _mod_call(c_all, w_mod, b_mod)
    mod4 = mod_all.reshape(DEPTH, N_ALL, 3, D_MODEL)

    bsb = jnp.repeat(jnp.swapaxes(b_bs, 1, 2), HEAD_W, axis=2)
    wss = jnp.repeat(jnp.transpose(b_ws[:, :, :S, :S], (0, 2, 3, 1)), HEAD_W, axis=3)
    bss = bsb[:, :S, :]

    xp = x_prompt.reshape(BATCH * SEQ, D_MODEL)
    ca_p, cc_p = [], []
    for l in range(DEPTH):
        xp, ca, cc = _prompt_layer(l, xp, mod4, g_pre3, g_post3, w_in_b, w_out_b, a_conv_w, acb3,
                                   alng3, alnb3, blng3, blnb3, b_ws, bsb, c_conv_w)
        ca_p.append(ca)
        cc_p.append(cc)

    x_t = jnp.swapaxes(x_sample, 0, 1)
    ha_t = jnp.swapaxes(state_conv_a, 1, 2)
    hc_t = jnp.swapaxes(state_conv_c, 1, 2)
    y_t, ca_t, cc_t, v_t = _sample_layers(
        x_t, mod_all, g_pre3, g_post3, w_in_b, w_out_b, a_conv_w, acb3,
        alng3, alnb3, blng3, blnb3, wss, bss, c_conv_w, ha_t, hc_t)

    return (xp.reshape(BATCH, SEQ, D_MODEL),
            jnp.swapaxes(y_t, 0, 1),
            jnp.stack(ca_p), jnp.stack(cc_p),
            jnp.swapaxes(ca_t, 1, 2), jnp.swapaxes(cc_t, 1, 2), jnp.swapaxes(v_t, 1, 2))
```

```python
import functools

import jax
import jax.numpy as jnp
from jax import lax
from jax.experimental import pallas as pl
from jax.experimental.pallas import tpu as pltpu

D_MODEL = 1024
BATCH = 8
SEQ = 2048
DEPTH = 4
DEC_BATCH = 128
DEC_SEQ = 4
W_A = 512
W_B = 512
W_C = 512
D_MIX = W_A + W_B + W_C
P_IN = 3 * W_A + 3 * W_B + 4 * W_C
CONV_A = 31
CONV_C = 3
CHUNK = 128
HEADS_B = 4
HEAD_W = W_B // HEADS_B
RMS_EPS = 1e-6
LN_EPS = 1e-5

LANES = 128
SUBLANES = 8
N_LANE_CHUNKS = W_A // LANES

COL_A = 0
COL_B = 3 * W_A
COL_C = 3 * W_A + 3 * W_B

TILE_T = 256
HIST_A = 32
HIST_C = 8
CONV_ROWS = 64
N_ALL = DEC_BATCH + BATCH
SAMPLE_N = 64

VMEM_LIMIT = 56 * 1024 * 1024

F32 = jnp.float32
BF16 = jnp.bfloat16


def _silu(x):
    return x * jax.nn.sigmoid(x)


def _layernorm(x, g, b):
    mu = jnp.mean(x, axis=-1, keepdims=True)
    xc = x - mu
    var = jnp.mean(xc * xc, axis=-1, keepdims=True)
    return xc * lax.rsqrt(var + LN_EPS) * g + b


def _mod_kernel(c_ref, w_ref, b_ref, o_ref):
    c = c_ref[...]
    s = _silu(c).astype(BF16)
    o_ref[...] = jnp.dot(s, w_ref[...].astype(BF16), preferred_element_type=F32) + b_ref[...]


def _mod_call(c_all, w_mod, b_mod):
    tn = D_MODEL
    return pl.pallas_call(
        _mod_kernel,
        grid=(DEPTH, 3 * D_MODEL // tn),
        in_specs=[
            pl.BlockSpec((N_ALL, D_MODEL), lambda l, n: (0, 0)),
            pl.BlockSpec((None, D_MODEL, tn), lambda l, n: (l, 0, n)),
            pl.BlockSpec((None, 1, tn), lambda l, n: (l, 0, n)),
        ],
        out_specs=pl.BlockSpec((None, N_ALL, tn), lambda l, n: (l, 0, n)),
        out_shape=jax.ShapeDtypeStruct((DEPTH, N_ALL, 3 * D_MODEL), F32),
        compiler_params=pltpu.CompilerParams(
            dimension_semantics=("arbitrary", "arbitrary"), vmem_limit_bytes=VMEM_LIMIT),
        name="mod",
    )(c_all, w_mod, b_mod.reshape(DEPTH, 1, 3 * D_MODEL))


def _prompt_kernel(layer, x_ref, mod_ref, gpre_ref, gpost_ref, win_hbm, wout_hbm,
                   acw_ref, acb_ref, alng_ref, alnb_ref, blng_ref, blnb_ref,
                   ws_ref, bsb_ref, ccw_ref,
                   y_ref, ca_ref, cc_ref,
                   win_ref, wout_ref, w_sem, hb_ref, p_ref, aext_ref, cext_ref, aconv_ref, vb_ref, mix_ref):
    T = TILE_T
    j = pl.program_id(1)
    off_a = HIST_A - (CONV_A - 1)
    off_c = HIST_C - (CONV_C - 1)

    @pl.when((pl.program_id(0) == 0) & (j == 0))
    def _():
        cp_in = pltpu.make_async_copy(win_hbm.at[layer], win_ref, w_sem.at[0])
        cp_out = pltpu.make_async_copy(wout_hbm.at[layer], wout_ref, w_sem.at[1])
        cp_in.start()
        cp_out.start()
        cp_in.wait()
        cp_out.wait()

    @pl.when(j == 0)
    def _():
        aext_ref[:, 0:HIST_A, :] = jnp.zeros((N_LANE_CHUNKS, HIST_A, LANES), F32)
        cext_ref[:, 0:HIST_C, :] = jnp.zeros((N_LANE_CHUNKS, HIST_C, LANES), F32)

    shift = mod_ref[0:1, :]
    scale = mod_ref[1:2, :]
    gate = mod_ref[2:3, :]
    x = x_ref[...]
    r = lax.rsqrt(jnp.mean(x * x, axis=-1, keepdims=True) + RMS_EPS)
    hb_ref[...] = ((x * r) * (gpre_ref[...] * (1.0 + scale)) + shift).astype(BF16)

    def proj(col):
        p_ref[:, col:col + W_A] = jnp.dot(hb_ref[...], win_ref[:, col:col + W_A], preferred_element_type=F32)

    def pcol(col):
        return p_ref[:, col:col + W_A]

    def conv_a_chunk(c):
        lanes = slice(c * LANES, (c + 1) * LANES)
        for r0 in range(0, T, CONV_ROWS):
            acc = jnp.broadcast_to(acb_ref[:, lanes], (CONV_ROWS, LANES))
            for k in range(CONV_A):
                s0 = r0 + off_a + k
                acc = acc + aext_ref[c, s0:s0 + CONV_ROWS, :] * acw_ref[k:k + 1, lanes]
            aconv_ref[r0:r0 + CONV_ROWS, lanes] = acc

    proj(COL_A)
    proj(COL_A + W_A)
    a = pcol(COL_A) * jax.nn.sigmoid(pcol(COL_A + W_A))
    for c in range(N_LANE_CHUNKS):
        aext_ref[c, HIST_A:HIST_A + T, :] = a[:, c * LANES:(c + 1) * LANES]
    proj(COL_B + W_B)
    conv_a_chunk(0)
    proj(COL_A + 2 * W_A)
    conv_a_chunk(1)
    proj(COL_B)
    conv_a_chunk(2)
    proj(COL_B + 2 * W_B)
    conv_a_chunk(3)
    proj(COL_C + W_C)
    y_a = _silu(_layernorm(aconv_ref[...], alng_ref[...], alnb_ref[...])) * _silu(pcol(COL_A + 2 * W_A))
    mix_ref[:, 0:W_A] = y_a.astype(BF16)
    proj(COL_C + 2 * W_C)

    vb_ref[...] = _layernorm(pcol(COL_B + W_B), blng_ref[...], blnb_ref[...]).astype(BF16)
    proj(COL_C)
    row_i = lax.broadcasted_iota(jnp.int32, (CHUNK, CHUNK), 0)
    col_i = lax.broadcasted_iota(jnp.int32, (CHUNK, CHUNK), 1)
    wm = [jnp.where(row_i >= col_i, ws_ref[hd], 0.0).astype(BF16) for hd in range(HEADS_B)]
    col_s = COL_B + W_B
    for ch in range(T // CHUNK):
        rows = slice(ch * CHUNK, (ch + 1) * CHUNK)
        for hd in range(HEADS_B):
            hcols = slice(hd * HEAD_W, (hd + 1) * HEAD_W)
            p_ref[rows, col_s + hd * HEAD_W:col_s + (hd + 1) * HEAD_W] = (
                jnp.dot(wm[hd], vb_ref[rows, hcols], preferred_element_type=F32) + bsb_ref[:, hcols])

    cx = pcol(COL_C + W_C) * pcol(COL_C + 2 * W_C)
    for c in range(N_LANE_CHUNKS):
        cext_ref[c, HIST_C:HIST_C + T, :] = cx[:, c * LANES:(c + 1) * LANES]
    proj(COL_C + 3 * W_C)
    y_b = pcol(COL_B) * pcol(col_s) * _silu(pcol(COL_B + 2 * W_B))
    mix_ref[:, W_A:W_A + W_B] = y_b.astype(BF16)
    for c in range(N_LANE_CHUNKS):
        lanes = slice(c * LANES, (c + 1) * LANES)
        acc = cext_ref[c, off_c:off_c + T, :] * ccw_ref[0:1, lanes]
        for k in range(1, CONV_C):
            acc = acc + cext_ref[c, off_c + k:off_c + k + T, :] * ccw_ref[k:k + 1, lanes]
        y_c = p_ref[:, COL_C + c * LANES:COL_C + (c + 1) * LANES] * acc * _silu(
            p_ref[:, COL_C + 3 * W_C + c * LANES:COL_C + 3 * W_C + (c + 1) * LANES])
        mix_ref[:, W_A + W_B + c * LANES:W_A + W_B + (c + 1) * LANES] = y_c.astype(BF16)

    p_ref[:, 0:D_MODEL] = jnp.dot(mix_ref[...], wout_ref[...], preferred_element_type=F32)
    mix = p_ref[:, 0:D_MODEL]
    r2 = lax.rsqrt(jnp.mean(mix * mix, axis=-1, keepdims=True) + RMS_EPS)
    y_ref[...] = x_ref[...] + (mix * r2) * (gate * gpost_ref[...])

    @pl.when(j == pl.num_programs(1) - 1)
    def _():
        for c in range(N_LANE_CHUNKS):
            ca_ref[:, c * LANES:(c + 1) * LANES] = aext_ref[c, T + off_a:T + HIST_A, :]
            cc_ref[:, c * LANES:(c + 1) * LANES] = cext_ref[c, T + off_c:T + HIST_C, :]

    for c in range(N_LANE_CHUNKS):
        aext_ref[c, 0:HIST_A, :] = aext_ref[c, T:T + HIST_A, :]
        cext_ref[c, 0:HIST_C, :] = cext_ref[c, T:T + HIST_C, :]


def _prompt_layer(l, x, mod4, g_pre, g_post, w_in, w_out, a_conv_w, a_conv_b,
                  a_ln_g, a_ln_b, b_ln_g, b_ln_b, b_ws, bsb, c_conv_w):
    nt = SEQ // TILE_T
    row = lambda w: pl.BlockSpec((None, 1, w), lambda b, j: (l, 0, 0))
    return pl.pallas_call(
        functools.partial(_prompt_kernel, l),
        grid=(BATCH, nt),
        in_specs=[
            pl.BlockSpec((TILE_T, D_MODEL), lambda b, j: (b * nt + j, 0)),
            pl.BlockSpec((None, None, 3, D_MODEL), lambda b, j: (l, DEC_BATCH + b, 0, 0)),
            row(D_MODEL), row(D_MODEL),
            pl.BlockSpec(memory_space=pl.ANY),
            pl.BlockSpec(memory_space=pl.ANY),
            pl.BlockSpec((None, CONV_A, W_A), lambda b, j: (l, 0, 0)),
            row(W_A), row(W_A), row(W_A), row(W_B), row(W_B),
            pl.BlockSpec((None, HEADS_B, CHUNK, CHUNK), lambda b, j: (l, 0, 0, 0)),
            pl.BlockSpec((None, CHUNK, W_B), lambda b, j: (l, 0, 0)),
            pl.BlockSpec((None, CONV_C, W_C), lambda b, j: (l, 0, 0)),
        ],
        out_specs=[
            pl.BlockSpec((TILE_T, D_MODEL), lambda b, j: (b * nt + j, 0)),
            pl.BlockSpec((None, CONV_A - 1, W_A), lambda b, j: (b, 0, 0)),
            pl.BlockSpec((None, CONV_C - 1, W_C), lambda b, j: (b, 0, 0)),
        ],
        out_shape=[
            jax.ShapeDtypeStruct((BATCH * SEQ, D_MODEL), F32),
            jax.ShapeDtypeStruct((BATCH, CONV_A - 1, W_A), F32),
            jax.ShapeDtypeStruct((BATCH, CONV_C - 1, W_C), F32),
        ],
        scratch_shapes=[
            pltpu.VMEM((D_MODEL, P_IN), BF16),
            pltpu.VMEM((D_MIX, D_MODEL), BF16),
            pltpu.SemaphoreType.DMA((2,)),
            pltpu.VMEM((TILE_T, D_MODEL), BF16),
            pltpu.VMEM((TILE_T, P_IN), F32),
            pltpu.VMEM((N_LANE_CHUNKS, TILE_T + HIST_A, LANES), F32),
            pltpu.VMEM((N_LANE_CHUNKS, TILE_T + HIST_C, LANES), F32),
            pltpu.VMEM((TILE_T, W_A), F32),
            pltpu.VMEM((TILE_T, W_B), BF16),
            pltpu.VMEM((TILE_T, D_MIX), BF16),
        ],
        compiler_params=pltpu.CompilerParams(
            dimension_semantics=("arbitrary", "arbitrary"), vmem_limit_bytes=VMEM_LIMIT),
        name=f"prompt_layer{l}",
    )(x, mod4, g_pre, g_post, w_in, w_out, a_conv_w, a_conv_b,
      a_ln_g, a_ln_b, b_ln_g, b_ln_b, b_ws, bsb, c_conv_w)


def _sample_kernel(x_ref, mod_ref, gpre_ref, gpost_ref, win_ref, wout_ref,
                   acw_ref, acb_ref, alng_ref, alnb_ref, blng_ref, blnb_ref,
                   wss_ref, bss_ref, ccw_ref, ha_ref, hc_ref,
                   y_ref, ca_ref, cc_ref, v_ref,
                   mix_ref):
    S, N = DEC_SEQ, SAMPLE_N
    l = pl.program_id(1)

    @pl.when(l == 0)
    def _():
        y_ref[...] = x_ref[...]

    shift = mod_ref[:, 0:D_MODEL]
    scale = mod_ref[:, D_MODEL:2 * D_MODEL]
    gate = mod_ref[:, 2 * D_MODEL:3 * D_MODEL]
    pre = gpre_ref[...] * (1.0 + scale)
    xs = [y_ref[t] for t in range(S)]
    hs = []
    for t in range(S):
        r = lax.rsqrt(jnp.mean(xs[t] * xs[t], axis=-1, keepdims=True) + RMS_EPS)
        hs.append(((xs[t] * r) * pre + shift).astype(BF16))
    hb = jnp.concatenate(hs, axis=0)

    pa = jnp.dot(hb, win_ref[:, COL_A:COL_A + 3 * W_A], preferred_element_type=F32)
    a = pa[:, 0:W_A] * jax.nn.sigmoid(pa[:, W_A:2 * W_A])
    a_z = pa[:, 2 * W_A:3 * W_A]
    n_hist = CONV_A - 1
    a_ext = lambda i: ha_ref[i] if i < n_hist else a[(i - n_hist) * N:(i - n_hist + 1) * N]
    for k in range(n_hist):
        ca_ref[k] = a_ext(k + S)
    for t in range(S):
        acc = jnp.broadcast_to(acb_ref[...], (N, W_A))
        for k in range(CONV_A):
            acc = acc + a_ext(t + k) * acw_ref[k:k + 1, :]
        ln = _layernorm(acc, alng_ref[...], alnb_ref[...])
        y_a = _silu(ln) * _silu(a_z[t * N:(t + 1) * N])
        mix_ref[t * N:(t + 1) * N, 0:W_A] = y_a.astype(BF16)

    pb = jnp.dot(hb, win_ref[:, COL_B:COL_B + 3 * W_B], preferred_element_type=F32)
    v_n = _layernorm(pb[:, W_B:2 * W_B], blng_ref[...], blnb_ref[...])
    vs = [v_n[t * N:(t + 1) * N] for t in range(S)]
    for t in range(S):
        v_ref[t] = vs[t]
        s = jnp.broadcast_to(bss_ref[t:t + 1, :], (N, W_B))
        for u in range(t + 1):
            s = s + vs[u] * wss_ref[t, u:u + 1, :]
        rows = slice(t * N, (t + 1) * N)
        y_b = pb[rows, 0:W_B] * s * _silu(pb[rows, 2 * W_B:3 * W_B])
        mix_ref[rows, W_A:W_A + W_B] = y_b.astype(BF16)

    pc = jnp.dot(hb, win_ref[:, COL_C:COL_C + 4 * W_C], preferred_element_type=F32)
    cx = pc[:, W_C:2 * W_C] * pc[:, 2 * W_C:3 * W_C]
    n_hc = CONV_C - 1
    c_ext = lambda i: hc_ref[i] if i < n_hc else cx[(i - n_hc) * N:(i - n_hc + 1) * N]
    for k in range(n_hc):
        cc_ref[k] = c_ext(k + S)
    for t in range(S):
        acc = c_ext(t) * ccw_ref[0:1, :]
        for k in range(1, CONV_C):
            acc = acc + c_ext(t + k) * ccw_ref[k:k + 1, :]
        rows = slice(t * N, (t + 1) * N)
        y_c = pc[rows, 0:W_C] * acc * _silu(pc[rows, 3 * W_C:4 * W_C])
        mix_ref[rows, W_A + W_B:D_MIX] = y_c.astype(BF16)

    mix = jnp.dot(mix_ref[...], wout_ref[...], preferred_element_type=F32)
    r2 = lax.rsqrt(jnp.mean(mix * mix, axis=-1, keepdims=True) + RMS_EPS)
    post = gate * gpost_ref[...]
    for t in range(S):
        rows = slice(t * N, (t + 1) * N)
        y_ref[t] = xs[t] + (mix[rows] * r2[rows]) * post


def _sample_layers(x_t, mod_all, g_pre, g_post, w_in, w_out, a_conv_w, a_conv_b,
                   a_ln_g, a_ln_b, b_ln_g, b_ln_b, wss, bss, c_conv_w, ha_t, hc_t):
    S, N = DEC_SEQ, SAMPLE_N
    nsplit = DEC_BATCH // N
    row = lambda w: pl.BlockSpec((None, 1, w), lambda n, l: (l, 0, 0))
    lay3 = lambda a, b: pl.BlockSpec((None, a, b), lambda n, l: (l, 0, 0))
    lay4 = lambda a, b, c: pl.BlockSpec((None, a, b, c), lambda n, l: (l, 0, 0, 0))
    seq4 = lambda a, c: pl.BlockSpec((None, a, N, c), lambda n, l: (l, 0, n, 0))
    return pl.pallas_call(
        _sample_kernel,
        grid=(nsplit, DEPTH),
        in_specs=[
            pl.BlockSpec((S, N, D_MODEL), lambda n, l: (0, n, 0)),
            pl.BlockSpec((None, N, 3 * D_MODEL), lambda n, l: (l, n, 0)),
            row(D_MODEL), row(D_MODEL),
            lay3(D_MODEL, P_IN), lay3(D_MIX, D_MODEL),
            lay3(CONV_A, W_A),
            row(W_A), row(W_A), row(W_A), row(W_B), row(W_B),
            lay4(S, S, W_B), lay3(S, W_B), lay3(CONV_C, W_C),
            seq4(CONV_A - 1, W_A), seq4(CONV_C - 1, W_C),
        ],
        out_specs=[
            pl.BlockSpec((S, N, D_MODEL), lambda n, l: (0, n, 0)),
            seq4(CONV_A - 1, W_A), seq4(CONV_C - 1, W_C), seq4(S, W_B),
        ],
        out_shape=[
            jax.ShapeDtypeStruct((S, DEC_BATCH, D_MODEL), F32),
            jax.ShapeDtypeStruct((DEPTH, CONV_A - 1, DEC_BATCH, W_A), F32),
            jax.ShapeDtypeStruct((DEPTH, CONV_C - 1, DEC_BATCH, W_C), F32),
            jax.ShapeDtypeStruct((DEPTH, S, DEC_BATCH, W_B), F32),
        ],
        scratch_shapes=[pltpu.VMEM((S * N, D_MIX), BF16)],
        compiler_params=pltpu.CompilerParams(
            dimension_semantics=("arbitrary", "arbitrary"), vmem_limit_bytes=VMEM_LIMIT),
        name="sample_layers",
    )(x_t, mod_all, g_pre, g_post, w_in, w_out, a_conv_w, a_conv_b,
      a_ln_g, a_ln_b, b_ln_g, b_ln_b, wss, bss, c_conv_w, ha_t, hc_t)


def kernel(x_prompt, x_sample, c_prompt, c_sample, state_conv_a, state_conv_c, w_mod, b_mod,
           g_pre, g_post, w_in, w_out, a_conv_w, a_conv_b, a_ln_g, a_ln_b, b_ln_g, b_ln_b,
           b_ws, b_bs, c_conv_w):
    S = DEC_SEQ
    w_in_b = w_in.astype(BF16)
    w_out_b = w_out.astype(BF16)
    r3 = lambda p: p.reshape(DEPTH, 1, p.shape[-1])
    g_pre3, g_post3 = r3(g_pre), r3(g_post)
    acb3, alng3, alnb3, blng3, blnb3 = r3(a_conv_b), r3(a_ln_g), r3(a_ln_b), r3(b_ln_g), r3(b_ln_b)

    c_all = jnp.concatenate([c_sample, c_prompt], axis=0)
    mod_all = _mod_call(c_all, w_mod, b_mod)
    mod4 = mod_all.reshape(DEPTH, N_ALL, 3, D_MODEL)

    bsb = jnp.repeat(jnp.swapaxes(b_bs, 1, 2), HEAD_W, axis=2)
    wss = jnp.repeat(jnp.transpose(b_ws[:, :, :S, :S], (0, 2, 3, 1)), HEAD_W, axis=3)
    bss = bsb[:, :S, :]

    xp = x_prompt.reshape(BATCH * SEQ, D_MODEL)
    ca_p, cc_p = [], []
    for l in range(DEPTH):
        xp, ca, cc = _prompt_layer(l, xp, mod4, g_pre3, g_post3, w_in_b, w_out_b, a_conv_w, acb3,
                                   alng3, alnb3, blng3, blnb3, b_ws, bsb, c_conv_w)
        ca_p.append(ca)
        cc_p.append(cc)

    x_t = jnp.swapaxes(x_sample, 0, 1)
    ha_t = jnp.swapaxes(state_conv_a, 1, 2)
    hc_t = jnp.swapaxes(state_conv_c, 1, 2)
    y_t, ca_t, cc_t, v_t = _sample_layers(
        x_t, mod_all, g_pre3, g_post3, w_in_b, w_out_b, a_conv_w, acb3,
        alng3, alnb3, blng3, blnb3, wss, bss, c_conv_w, ha_t, hc_t)

    return (xp.reshape(BATCH, SEQ, D_MODEL),
            jnp.swapaxes(y_t, 0, 1),
            jnp.stack(ca_p), jnp.stack(cc_p),
            jnp.swapaxes(ca_t, 1, 2), jnp.swapaxes(cc_t, 1, 2), jnp.swapaxes(v_t, 1, 2))
```

```python
import functools

import jax
import jax.numpy as jnp
from jax import lax
from jax.experimental import pallas as pl
from jax.experimental.pallas import tpu as pltpu

D_MODEL = 1024
BATCH = 8
SEQ = 2048
DEPTH = 4
DEC_BATCH = 128
DEC_SEQ = 4
W_A = 512
W_B = 512
W_C = 512
D_MIX = W_A + W_B + W_C
P_IN = 3 * W_A + 3 * W_B + 4 * W_C
CONV_A = 31
CONV_C = 3
CHUNK = 128
HEADS_B = 4
HEAD_W = W_B // HEADS_B
RMS_EPS = 1e-6
LN_EPS = 1e-5

LANES = 128
SUBLANES = 8
N_LANE_CHUNKS = W_A // LANES

COL_A = 0
COL_B = 3 * W_A
COL_C = 3 * W_A + 3 * W_B

TILE_T = 512
HIST_A = 32
HIST_C = 8
CONV_ROWS = 16
N_ALL = DEC_BATCH + BATCH
SAMPLE_N = 64

VMEM_LIMIT = 56 * 1024 * 1024

F32 = jnp.float32
BF16 = jnp.bfloat16


def _silu(x):
    return x * jax.nn.sigmoid(x)


def _layernorm(x, g, b):
    mu = jnp.mean(x, axis=-1, keepdims=True)
    xc = x - mu
    var = jnp.mean(xc * xc, axis=-1, keepdims=True)
    return xc * lax.rsqrt(var + LN_EPS) * g + b


def _mod_kernel(c_ref, w_ref, b_ref, o_ref):
    c = c_ref[...]
    s = _silu(c).astype(BF16)
    o_ref[...] = jnp.dot(s, w_ref[...].astype(BF16), preferred_element_type=F32) + b_ref[...]


def _mod_call(c_all, w_mod, b_mod):
    tn = D_MODEL
    return pl.pallas_call(
        _mod_kernel,
        grid=(DEPTH, 3 * D_MODEL // tn),
        in_specs=[
            pl.BlockSpec((N_ALL, D_MODEL), lambda l, n: (0, 0)),
            pl.BlockSpec((None, D_MODEL, tn), lambda l, n: (l, 0, n)),
            pl.BlockSpec((None, 1, tn), lambda l, n: (l, 0, n)),
        ],
        out_specs=pl.BlockSpec((None, N_ALL, tn), lambda l, n: (l, 0, n)),
        out_shape=jax.ShapeDtypeStruct((DEPTH, N_ALL, 3 * D_MODEL), F32),
        compiler_params=pltpu.CompilerParams(
            dimension_semantics=("arbitrary", "arbitrary"), vmem_limit_bytes=VMEM_LIMIT),
        name="mod",
    )(c_all, w_mod, b_mod.reshape(DEPTH, 1, 3 * D_MODEL))


def _prompt_kernel(layer, x_ref, mod_ref, gpre_ref, gpost_ref, win_hbm, wout_hbm,
                   acw_ref, acb_ref, alng_ref, alnb_ref, blng_ref, blnb_ref,
                   ws_ref, bsb_ref, ccw_ref,
                   y_ref, ca_ref, cc_ref,
                   win_ref, wout_ref, w_sem, hb_ref, p_ref, aext_ref, cext_ref, aconv_ref, vb_ref, mix_ref):
    T = TILE_T
    j = pl.program_id(1)
    off_a = HIST_A - (CONV_A - 1)
    off_c = HIST_C - (CONV_C - 1)

    @pl.when((pl.program_id(0) == 0) & (j == 0))
    def _():
        cp_in = pltpu.make_async_copy(win_hbm.at[layer], win_ref, w_sem.at[0])
        cp_out = pltpu.make_async_copy(wout_hbm.at[layer], wout_ref, w_sem.at[1])
        cp_in.start()
        cp_out.start()
        cp_in.wait()
        cp_out.wait()

    @pl.when(j == 0)
    def _():
        aext_ref[:, 0:HIST_A, :] = jnp.zeros((N_LANE_CHUNKS, HIST_A, LANES), F32)
        cext_ref[:, 0:HIST_C, :] = jnp.zeros((N_LANE_CHUNKS, HIST_C, LANES), F32)

    shift = mod_ref[0:1, :]
    scale = mod_ref[1:2, :]
    gate = mod_ref[2:3, :]
    x = x_ref[...]
    r = lax.rsqrt(jnp.mean(x * x, axis=-1, keepdims=True) + RMS_EPS)
    hb_ref[...] = ((x * r) * (gpre_ref[...] * (1.0 + scale)) + shift).astype(BF16)

    def proj(col):
        p_ref[:, col:col + W_A] = jnp.dot(hb_ref[...], win_ref[:, col:col + W_A], preferred_element_type=F32)

    def pcol(col):
        return p_ref[:, col:col + W_A]

    def conv_a_chunk(c):
        lanes = slice(c * LANES, (c + 1) * LANES)
        for r0 in range(0, T, CONV_ROWS):
            acc = jnp.broadcast_to(acb_ref[:, lanes], (CONV_ROWS, LANES))
            for k in range(CONV_A):
                s0 = r0 + off_a + k
                acc = acc + aext_ref[c, s0:s0 + CONV_ROWS, :] * acw_ref[k:k + 1, lanes]
            aconv_ref[r0:r0 + CONV_ROWS, lanes] = acc

    proj(COL_A)
    proj(COL_A + W_A)
    a = pcol(COL_A) * jax.nn.sigmoid(pcol(COL_A + W_A))
    for c in range(N_LANE_CHUNKS):
        aext_ref[c, HIST_A:HIST_A + T, :] = a[:, c * LANES:(c + 1) * LANES]
    proj(COL_A + 2 * W_A)
    conv_a_chunk(0)

    proj(COL_B + W_B)
    proj(COL_B)
    proj(COL_B + 2 * W_B)
    vb_ref[...] = _layernorm(pcol(COL_B + W_B), blng_ref[...], blnb_ref[...]).astype(BF16)
    conv_a_chunk(1)
    row_i = lax.broadcasted_iota(jnp.int32, (CHUNK, CHUNK), 0)
    col_i = lax.broadcasted_iota(jnp.int32, (CHUNK, CHUNK), 1)
    wm = [jnp.where(row_i >= col_i, ws_ref[hd], 0.0).astype(BF16) for hd in range(HEADS_B)]
    col_s = COL_B + W_B
    for ch in range(T // CHUNK):
        rows = slice(ch * CHUNK, (ch + 1) * CHUNK)
        for hd in range(HEADS_B):
            hcols = slice(hd * HEAD_W, (hd + 1) * HEAD_W)
            p_ref[rows, col_s + hd * HEAD_W:col_s + (hd + 1) * HEAD_W] = (
                jnp.dot(wm[hd], vb_ref[rows, hcols], preferred_element_type=F32) + bsb_ref[:, hcols])
    y_b = pcol(COL_B) * pcol(col_s) * _silu(pcol(COL_B + 2 * W_B))
    mix_ref[:, W_A:W_A + W_B] = y_b.astype(BF16)

    proj(COL_C + W_C)
    proj(COL_C + 2 * W_C)
    conv_a_chunk(2)
    cx = pcol(COL_C + W_C) * pcol(COL_C + 2 * W_C)
    for c in range(N_LANE_CHUNKS):
        cext_ref[c, HIST_C:HIST_C + T, :] = cx[:, c * LANES:(c + 1) * LANES]
    proj(COL_C)
    proj(COL_C + 3 * W_C)
    conv_a_chunk(3)
    for c in range(N_LANE_CHUNKS):
        lanes = slice(c * LANES, (c + 1) * LANES)
        acc = cext_ref[c, off_c:off_c + T, :] * ccw_ref[0:1, lanes]
        for k in range(1, CONV_C):
            acc = acc + cext_ref[c, off_c + k:off_c + k + T, :] * ccw_ref[k:k + 1, lanes]
        y_c = p_ref[:, COL_C + c * LANES:COL_C + (c + 1) * LANES] * acc * _silu(
            p_ref[:, COL_C + 3 * W_C + c * LANES:COL_C + 3 * W_C + (c + 1) * LANES])
        mix_ref[:, W_A + W_B + c * LANES:W_A + W_B + (c + 1) * LANES] = y_c.astype(BF16)

    out_bc = jnp.dot(mix_ref[:, W_A:D_MIX], wout_ref[W_A:D_MIX, :], preferred_element_type=F32)
    y_a = _silu(_layernorm(aconv_ref[...], alng_ref[...], alnb_ref[...])) * _silu(pcol(COL_A + 2 * W_A))
    mix_ref[:, 0:W_A] = y_a.astype(BF16)
    mix = out_bc + jnp.dot(mix_ref[:, 0:W_A], wout_ref[0:W_A, :], preferred_element_type=F32)
    r2 = lax.rsqrt(jnp.mean(mix * mix, axis=-1, keepdims=True) + RMS_EPS)
    y_ref[...] = x_ref[...] + (mix * r2) * (gate * gpost_ref[...])

    @pl.when(j == pl.num_programs(1) - 1)
    def _():
        for c in range(N_LANE_CHUNKS):
            ca_ref[:, c * LANES:(c + 1) * LANES] = aext_ref[c, T + off_a:T + HIST_A, :]
            cc_ref[:, c * LANES:(c + 1) * LANES] = cext_ref[c, T + off_c:T + HIST_C, :]

    for c in range(N_LANE_CHUNKS):
        aext_ref[c, 0:HIST_A, :] = aext_ref[c, T:T + HIST_A, :]
        cext_ref[c, 0:HIST_C, :] = cext_ref[c, T:T + HIST_C, :]


def _prompt_layer(l, x, mod4, g_pre, g_post, w_in, w_out, a_conv_w, a_conv_b,
                  a_ln_g, a_ln_b, b_ln_g, b_ln_b, b_ws, bsb, c_conv_w):
    nt = SEQ // TILE_T
    row = lambda w: pl.BlockSpec((None, 1, w), lambda b, j: (l, 0, 0))
    return pl.pallas_call(
        functools.partial(_prompt_kernel, l),
        grid=(BATCH, nt),
        in_specs=[
            pl.BlockSpec((TILE_T, D_MODEL), lambda b, j: (b * nt + j, 0)),
            pl.BlockSpec((None, None, 3, D_MODEL), lambda b, j: (l, DEC_BATCH + b, 0, 0)),
            row(D_MODEL), row(D_MODEL),
            pl.BlockSpec(memory_space=pl.ANY),
            pl.BlockSpec(memory_space=pl.ANY),
            pl.BlockSpec((None, CONV_A, W_A), lambda b, j: (l, 0, 0)),
            row(W_A), row(W_A), row(W_A), row(W_B), row(W_B),
            pl.BlockSpec((None, HEADS_B, CHUNK, CHUNK), lambda b, j: (l, 0, 0, 0)),
            pl.BlockSpec((None, CHUNK, W_B), lambda b, j: (l, 0, 0)),
            pl.BlockSpec((None, CONV_C, W_C), lambda b, j: (l, 0, 0)),
        ],
        out_specs=[
            pl.BlockSpec((TILE_T, D_MODEL), lambda b, j: (b * nt + j, 0)),
            pl.BlockSpec((None, CONV_A - 1, W_A), lambda b, j: (b, 0, 0)),
            pl.BlockSpec((None, CONV_C - 1, W_C), lambda b, j: (b, 0, 0)),
        ],
        out_shape=[
            jax.ShapeDtypeStruct((BATCH * SEQ, D_MODEL), F32),
            jax.ShapeDtypeStruct((BATCH, CONV_A - 1, W_A), F32),
            jax.ShapeDtypeStruct((BATCH, CONV_C - 1, W_C), F32),
        ],
        scratch_shapes=[
            pltpu.VMEM((D_MODEL, P_IN), BF16),
            pltpu.VMEM((D_MIX, D_MODEL), BF16),
            pltpu.SemaphoreType.DMA((2,)),
            pltpu.VMEM((TILE_T, D_MODEL), BF16),
            pltpu.VMEM((TILE_T, P_IN), F32),
            pltpu.VMEM((N_LANE_CHUNKS, TILE_T + HIST_A, LANES), F32),
            pltpu.VMEM((N_LANE_CHUNKS, TILE_T + HIST_C, LANES), F32),
            pltpu.VMEM((TILE_T, W_A), F32),
            pltpu.VMEM((TILE_T, W_B), BF16),
            pltpu.VMEM((TILE_T, D_MIX), BF16),
        ],
        compiler_params=pltpu.CompilerParams(
            dimension_semantics=("arbitrary", "arbitrary"), vmem_limit_bytes=VMEM_LIMIT),
        name=f"prompt_layer{l}",
    )(x, mod4, g_pre, g_post, w_in, w_out, a_conv_w, a_conv_b,
      a_ln_g, a_ln_b, b_ln_g, b_ln_b, b_ws, bsb, c_conv_w)


def _sample_kernel(x_ref, mod_ref, gpre_ref, gpost_ref, win_ref, wout_ref,
                   acw_ref, acb_ref, alng_ref, alnb_ref, blng_ref, blnb_ref,
                   wss_ref, bss_ref, ccw_ref, ha_ref, hc_ref,
                   y_ref, ca_ref, cc_ref, v_ref,
                   mix_ref):
    S, N = DEC_SEQ, SAMPLE_N
    l = pl.program_id(1)

    @pl.when(l == 0)
    def _():
        y_ref[...] = x_ref[...]

    shift = mod_ref[:, 0:D_MODEL]
    scale = mod_ref[:, D_MODEL:2 * D_MODEL]
    gate = mod_ref[:, 2 * D_MODEL:3 * D_MODEL]
    pre = gpre_ref[...] * (1.0 + scale)
    xs = [y_ref[t] for t in range(S)]
    hs = []
    for t in range(S):
        r = lax.rsqrt(jnp.mean(xs[t] * xs[t], axis=-1, keepdims=True) + RMS_EPS)
        hs.append(((xs[t] * r) * pre + shift).astype(BF16))
    hb = jnp.concatenate(hs, axis=0)

    pa = jnp.dot(hb, win_ref[:, COL_A:COL_A + 3 * W_A], preferred_element_type=F32)
    a = pa[:, 0:W_A] * jax.nn.sigmoid(pa[:, W_A:2 * W_A])
    a_z = pa[:, 2 * W_A:3 * W_A]
    n_hist = CONV_A - 1
    a_ext = lambda i: ha_ref[i] if i < n_hist else a[(i - n_hist) * N:(i - n_hist + 1) * N]
    for k in range(n_hist):
        ca_ref[k] = a_ext(k + S)
    for t in range(S):
        acc = jnp.broadcast_to(acb_ref[...], (N, W_A))
        for k in range(CONV_A):
            acc = acc + a_ext(t + k) * acw_ref[k:k + 1, :]
        ln = _layernorm(acc, alng_ref[...], alnb_ref[...])
        y_a = _silu(ln) * _silu(a_z[t * N:(t + 1) * N])
        mix_ref[t * N:(t + 1) * N, 0:W_A] = y_a.astype(BF16)

    pb = jnp.dot(hb, win_ref[:, COL_B:COL_B + 3 * W_B], preferred_element_type=F32)
    v_n = _layernorm(pb[:, W_B:2 * W_B], blng_ref[...], blnb_ref[...])
    vs = [v_n[t * N:(t + 1) * N] for t in range(S)]
    for t in range(S):
        v_ref[t] = vs[t]
        s = jnp.broadcast_to(bss_ref[t:t + 1, :], (N, W_B))
        for u in range(t + 1):
            s = s + vs[u] * wss_ref[t, u:u + 1, :]
        rows = slice(t * N, (t + 1) * N)
        y_b = pb[rows, 0:W_B] * s * _silu(pb[rows, 2 * W_B:3 * W_B])
        mix_ref[rows, W_A:W_A + W_B] = y_b.astype(BF16)

    pc = jnp.dot(hb, win_ref[:, COL_C:COL_C + 4 * W_C], preferred_element_type=F32)
    cx = pc[:, W_C:2 * W_C] * pc[:, 2 * W_C:3 * W_C]
    n_hc = CONV_C - 1
    c_ext = lambda i: hc_ref[i] if i < n_hc else cx[(i - n_hc) * N:(i - n_hc + 1) * N]
    for k in range(n_hc):
        cc_ref[k] = c_ext(k + S)
    for t in range(S):
        acc = c_ext(t) * ccw_ref[0:1, :]
        for k in range(1, CONV_C):
            acc = acc + c_ext(t + k) * ccw_ref[k:k + 1, :]
        rows = slice(t * N, (t + 1) * N)
        y_c = pc[rows, 0:W_C] * acc * _silu(pc[rows, 3 * W_C:4 * W_C])
        mix_ref[rows, W_A + W_B:D_MIX] = y_c.astype(BF16)

    mix = jnp.dot(mix_ref[...], wout_ref[...], preferred_element_type=F32)
    r2 = lax.rsqrt(jnp.mean(mix * mix, axis=-1, keepdims=True) + RMS_EPS)
    post = gate * gpost_ref[...]
    for t in range(S):
        rows = slice(t * N, (t + 1) * N)
        y_ref[t] = xs[t] + (mix[rows] * r2[rows]) * post


def _sample_layers(x_t, mod_all, g_pre, g_post, w_in, w_out, a_conv_w, a_conv_b,
                   a_ln_g, a_ln_b, b_ln_g, b_ln_b, wss, bss, c_conv_w, ha_t, hc_t):
    S, N = DEC_SEQ, SAMPLE_N
    nsplit = DEC_BATCH // N
    row = lambda w: pl.BlockSpec((None, 1, w), lambda n, l: (l, 0, 0))
    lay3 = lambda a, b: pl.BlockSpec((None, a, b), lambda n, l: (l, 0, 0))
    lay4 = lambda a, b, c: pl.BlockSpec((None, a, b, c), lambda n, l: (l, 0, 0, 0))
    seq4 = lambda a, c: pl.BlockSpec((None, a, N, c), lambda n, l: (l, 0, n, 0))
    return pl.pallas_call(
        _sample_kernel,
        grid=(nsplit, DEPTH),
        in_specs=[
            pl.BlockSpec((S, N, D_MODEL), lambda n, l: (0, n, 0)),
            pl.BlockSpec((None, N, 3 * D_MODEL), lambda n, l: (l, n, 0)),
            row(D_MODEL), row(D_MODEL),
            lay3(D_MODEL, P_IN), lay3(D_MIX, D_MODEL),
            lay3(CONV_A, W_A),
            row(W_A), row(W_A), row(W_A), row(W_B), row(W_B),
            lay4(S, S, W_B), lay3(S, W_B), lay3(CONV_C, W_C),
            seq4(CONV_A - 1, W_A), seq4(CONV_C - 1, W_C),
        ],
        out_specs=[
            pl.BlockSpec((S, N, D_MODEL), lambda n, l: (0, n, 0)),
            seq4(CONV_A - 1, W_A), seq4(CONV_C - 1, W_C), seq4(S, W_B),
        ],
        out_shape=[
            jax.ShapeDtypeStruct((S, DEC_BATCH, D_MODEL), F32),
            jax.ShapeDtypeStruct((DEPTH, CONV_A - 1, DEC_BATCH, W_A), F32),
            jax.ShapeDtypeStruct((DEPTH, CONV_C - 1, DEC_BATCH, W_C), F32),
            jax.ShapeDtypeStruct((DEPTH, S, DEC_BATCH, W_B), F32),
        ],
        scratch_shapes=[pltpu.VMEM((S * N, D_MIX), BF16)],
        compiler_params=pltpu.CompilerParams(
            dimension_semantics=("arbitrary", "arbitrary"), vmem_limit_bytes=VMEM_LIMIT),
        name="sample_layers",
    )(x_t, mod_all, g_pre, g_post, w_in, w_out, a_conv_w, a_conv_b,
      a_ln_g, a_ln_b, b_ln_g, b_ln_b, wss, bss, c_conv_w, ha_t, hc_t)


def kernel(x_prompt, x_sample, c_prompt, c_sample, state_conv_a, state_conv_c, w_mod, b_mod,
           g_pre, g_post, w_in, w_out, a_conv_w, a_conv_b, a_ln_g, a_ln_b, b_ln_g, b_ln_b,
           b_ws, b_bs, c_conv_w):
    S = DEC_SEQ
    w_in_b = w_in.astype(BF16)
    w_out_b = w_out.astype(BF16)
    r3 = lambda p: p.reshape(DEPTH, 1, p.shape[-1])
    g_pre3, g_post3 = r3(g_pre), r3(g_post)
    acb3, alng3, alnb3, blng3, blnb3 = r3(a_conv_b), r3(a_ln_g), r3(a_ln_b), r3(b_ln_g), r3(b_ln_b)

    c_all = jnp.concatenate([c_sample, c_prompt], axis=0)
    mod_all = _mod_call(c_all, w_mod, b_mod)
    mod4 = mod_all.reshape(DEPTH, N_ALL, 3, D_MODEL)

    bsb = jnp.repeat(jnp.swapaxes(b_bs, 1, 2), HEAD_W, axis=2)
    wss = jnp.repeat(jnp.transpose(b_ws[:, :, :S, :S], (0, 2, 3, 1)), HEAD_W, axis=3)
    bss = bsb[:, :S, :]

    xp = x_prompt.reshape(BATCH * SEQ, D_MODEL)
    ca_p, cc_p = [], []
    for l in range(DEPTH):
        xp, ca, cc = _prompt_layer(l, xp, mod4, g_pre3, g_post3, w_in_b, w_out_b, a_conv_w, acb3,
                                   alng3, alnb3, blng3, blnb3, b_ws, bsb, c_conv_w)
        ca_p.append(ca)
        cc_p.append(cc)

    x_t = jnp.swapaxes(x_sample, 0, 1)
    ha_t = jnp.swapaxes(state_conv_a, 1, 2)
    hc_t = jnp.swapaxes(state_conv_c, 1, 2)
    y_t, ca_t, cc_t, v_t = _sample_layers(
        x_t, mod_all, g_pre3, g_post3, w_in_b, w_out_b, a_conv_w, acb3,
        alng3, alnb3, blng3, blnb3, wss, bss, c_conv_w, ha_t, hc_t)

    return (xp.reshape(BATCH, SEQ, D_MODEL),
            jnp.swapaxes(y_t, 0, 1),
            jnp.stack(ca_p), jnp.stack(cc_p),
            jnp.swapaxes(ca_t, 1, 2), jnp.swapaxes(cc_t, 1, 2), jnp.swapaxes(v_t, 1, 2))
```

```python
import functools

import jax
import jax.numpy as jnp
from jax import lax
from jax.experimental import pallas as pl
from jax.experimental.pallas import tpu as pltpu

D_MODEL = 1024
BATCH = 8
SEQ = 2048
DEPTH = 4
DEC_BATCH = 128
DEC_SEQ = 4
W_A = 512
W_B = 512
W_C = 512
D_MIX = W_A + W_B + W_C
P_IN = 3 * W_A + 3 * W_B + 4 * W_C
CONV_A = 31
CONV_C = 3
CHUNK = 128
HEADS_B = 4
HEAD_W = W_B // HEADS_B
RMS_EPS = 1e-6
LN_EPS = 1e-5

LANES = 128
SUBLANES = 8
N_LANE_CHUNKS = W_A // LANES

COL_A = 0
COL_B = 3 * W_A
COL_C = 3 * W_A + 3 * W_B

TILE_T = 512
HIST_A = 32
HIST_C = 8
CONV_ROWS = 16
EDGE_ROWS = TILE_T // 4
N_ALL = DEC_BATCH + BATCH
SAMPLE_N = 64

VMEM_LIMIT = 56 * 1024 * 1024

F32 = jnp.float32
BF16 = jnp.bfloat16


def _silu(x):
    return x * jax.nn.sigmoid(x)


def _layernorm(x, g, b):
    mu = jnp.mean(x, axis=-1, keepdims=True)
    xc = x - mu
    var = jnp.mean(xc * xc, axis=-1, keepdims=True)
    return xc * lax.rsqrt(var + LN_EPS) * g + b


def _mod_kernel(c_ref, w_ref, b_ref, o_ref):
    c = c_ref[...]
    s = _silu(c).astype(BF16)
    o_ref[...] = jnp.dot(s, w_ref[...].astype(BF16), preferred_element_type=F32) + b_ref[...]


def _mod_call(c_all, w_mod, b_mod):
    tn = D_MODEL
    return pl.pallas_call(
        _mod_kernel,
        grid=(DEPTH, 3 * D_MODEL // tn),
        in_specs=[
            pl.BlockSpec((N_ALL, D_MODEL), lambda l, n: (0, 0)),
            pl.BlockSpec((None, D_MODEL, tn), lambda l, n: (l, 0, n)),
            pl.BlockSpec((None, 1, tn), lambda l, n: (l, 0, n)),
        ],
        out_specs=pl.BlockSpec((None, N_ALL, tn), lambda l, n: (l, 0, n)),
        out_shape=jax.ShapeDtypeStruct((DEPTH, N_ALL, 3 * D_MODEL), F32),
        compiler_params=pltpu.CompilerParams(
            dimension_semantics=("arbitrary", "arbitrary"), vmem_limit_bytes=VMEM_LIMIT),
        name="mod",
    )(c_all, w_mod, b_mod.reshape(DEPTH, 1, 3 * D_MODEL))


def _prompt_kernel(layer, x_ref, mod_ref, gpre_ref, gpost_ref, win_hbm, wout_hbm,
                   acw_ref, acb_ref, alng_ref, alnb_ref, blng_ref, blnb_ref,
                   ws_ref, bsb_ref, ccw_ref,
                   y_ref, ca_ref, cc_ref,
                   win_ref, wout_ref, w_sem, hb_ref, p_ref, aext_ref, cext_ref, aconv_ref, vb_ref, mix_ref):
    T = TILE_T
    j = pl.program_id(1)
    off_a = HIST_A - (CONV_A - 1)
    off_c = HIST_C - (CONV_C - 1)

    @pl.when((pl.program_id(0) == 0) & (j == 0))
    def _():
        cp_in = pltpu.make_async_copy(win_hbm.at[layer], win_ref, w_sem.at[0])
        cp_out = pltpu.make_async_copy(wout_hbm.at[layer], wout_ref, w_sem.at[1])
        cp_in.start()
        cp_out.start()
        cp_in.wait()
        cp_out.wait()

    @pl.when(j == 0)
    def _():
        aext_ref[:, 0:HIST_A, :] = jnp.zeros((N_LANE_CHUNKS, HIST_A, LANES), F32)
        cext_ref[:, 0:HIST_C, :] = jnp.zeros((N_LANE_CHUNKS, HIST_C, LANES), F32)

    shift = mod_ref[0:1, :]
    scale = mod_ref[1:2, :]
    gate = mod_ref[2:3, :]
    pre = gpre_ref[...] * (1.0 + scale)
    for h0 in range(0, T, EDGE_ROWS):
        hr = slice(h0, h0 + EDGE_ROWS)
        x = x_ref[hr, :]
        r = lax.rsqrt(jnp.mean(x * x, axis=-1, keepdims=True) + RMS_EPS)
        hb_ref[hr, :] = ((x * r) * pre + shift).astype(BF16)

    def proj(col):
        p_ref[:, col:col + W_A] = jnp.dot(hb_ref[...], win_ref[:, col:col + W_A], preferred_element_type=F32)

    def pcol(col):
        return p_ref[:, col:col + W_A]

    def conv_a_chunk(c):
        lanes = slice(c * LANES, (c + 1) * LANES)
        for r0 in range(0, T, CONV_ROWS):
            acc = jnp.broadcast_to(acb_ref[:, lanes], (CONV_ROWS, LANES))
            for k in range(CONV_A):
                s0 = r0 + off_a + k
                acc = acc + aext_ref[c, s0:s0 + CONV_ROWS, :] * acw_ref[k:k + 1, lanes]
            aconv_ref[r0:r0 + CONV_ROWS, lanes] = acc

    for h0 in range(0, T, EDGE_ROWS):
        hr = slice(h0, h0 + EDGE_ROWS)
        for col in (COL_A, COL_A + W_A):
            p_ref[hr, col:col + W_A] = jnp.dot(hb_ref[hr, :], win_ref[:, col:col + W_A], preferred_element_type=F32)
    a = pcol(COL_A) * jax.nn.sigmoid(pcol(COL_A + W_A))
    for c in range(N_LANE_CHUNKS):
        aext_ref[c, HIST_A:HIST_A + T, :] = a[:, c * LANES:(c + 1) * LANES]
    proj(COL_A + 2 * W_A)
    conv_a_chunk(0)

    proj(COL_B + W_B)
    proj(COL_B)
    proj(COL_B + 2 * W_B)
    vb_ref[...] = _layernorm(pcol(COL_B + W_B), blng_ref[...], blnb_ref[...]).astype(BF16)
    conv_a_chunk(1)
    row_i = lax.broadcasted_iota(jnp.int32, (CHUNK, CHUNK), 0)
    col_i = lax.broadcasted_iota(jnp.int32, (CHUNK, CHUNK), 1)
    wm = [jnp.where(row_i >= col_i, ws_ref[hd], 0.0).astype(BF16) for hd in range(HEADS_B)]
    col_s = COL_B + W_B
    for ch in range(T // CHUNK):
        rows = slice(ch * CHUNK, (ch + 1) * CHUNK)
        for hd in range(HEADS_B):
            hcols = slice(hd * HEAD_W, (hd + 1) * HEAD_W)
            p_ref[rows, col_s + hd * HEAD_W:col_s + (hd + 1) * HEAD_W] = (
                jnp.dot(wm[hd], vb_ref[rows, hcols], preferred_element_type=F32) + bsb_ref[:, hcols])
    y_b = pcol(COL_B) * pcol(col_s) * _silu(pcol(COL_B + 2 * W_B))
    mix_ref[:, W_A:W_A + W_B] = y_b.astype(BF16)

    proj(COL_C + W_C)
    proj(COL_C + 2 * W_C)
    conv_a_chunk(2)
    cx = pcol(COL_C + W_C) * pcol(COL_C + 2 * W_C)
    for c in range(N_LANE_CHUNKS):
        cext_ref[c, HIST_C:HIST_C + T, :] = cx[:, c * LANES:(c + 1) * LANES]
    proj(COL_C)
    proj(COL_C + 3 * W_C)
    conv_a_chunk(3)
    for c in range(N_LANE_CHUNKS):
        lanes = slice(c * LANES, (c + 1) * LANES)
        acc = cext_ref[c, off_c:off_c + T, :] * ccw_ref[0:1, lanes]
        for k in range(1, CONV_C):
            acc = acc + cext_ref[c, off_c + k:off_c + k + T, :] * ccw_ref[k:k + 1, lanes]
        y_c = p_ref[:, COL_C + c * LANES:COL_C + (c + 1) * LANES] * acc * _silu(
            p_ref[:, COL_C + 3 * W_C + c * LANES:COL_C + 3 * W_C + (c + 1) * LANES])
        mix_ref[:, W_A + W_B + c * LANES:W_A + W_B + (c + 1) * LANES] = y_c.astype(BF16)

    out_bc = jnp.dot(mix_ref[:, W_A:D_MIX], wout_ref[W_A:D_MIX, :], preferred_element_type=F32)
    y_a = _silu(_layernorm(aconv_ref[...], alng_ref[...], alnb_ref[...])) * _silu(pcol(COL_A + 2 * W_A))
    mix_ref[:, 0:W_A] = y_a.astype(BF16)
    post = gate * gpost_ref[...]
    for h0 in range(0, T, EDGE_ROWS):
        hr = slice(h0, h0 + EDGE_ROWS)
        mix = out_bc[hr] + jnp.dot(mix_ref[hr, 0:W_A], wout_ref[0:W_A, :], preferred_element_type=F32)
        r2 = lax.rsqrt(jnp.mean(mix * mix, axis=-1, keepdims=True) + RMS_EPS)
        y_ref[hr, :] = x_ref[hr, :] + (mix * r2) * post

    @pl.when(j == pl.num_programs(1) - 1)
    def _():
        for c in range(N_LANE_CHUNKS):
            ca_ref[:, c * LANES:(c + 1) * LANES] = aext_ref[c, T + off_a:T + HIST_A, :]
            cc_ref[:, c * LANES:(c + 1) * LANES] = cext_ref[c, T + off_c:T + HIST_C, :]

    for c in range(N_LANE_CHUNKS):
        aext_ref[c, 0:HIST_A, :] = aext_ref[c, T:T + HIST_A, :]
        cext_ref[c, 0:HIST_C, :] = cext_ref[c, T:T + HIST_C, :]


def _prompt_layer(l, x, mod4, g_pre, g_post, w_in, w_out, a_conv_w, a_conv_b,
                  a_ln_g, a_ln_b, b_ln_g, b_ln_b, b_ws, bsb, c_conv_w):
    nt = SEQ // TILE_T
    row = lambda w: pl.BlockSpec((None, 1, w), lambda b, j: (l, 0, 0))
    return pl.pallas_call(
        functools.partial(_prompt_kernel, l),
        grid=(BATCH, nt),
        in_specs=[
            pl.BlockSpec((TILE_T, D_MODEL), lambda b, j: (b * nt + j, 0)),
            pl.BlockSpec((None, None, 3, D_MODEL), lambda b, j: (l, DEC_BATCH + b, 0, 0)),
            row(D_MODEL), row(D_MODEL),
            pl.BlockSpec(memory_space=pl.ANY),
            pl.BlockSpec(memory_space=pl.ANY),
            pl.BlockSpec((None, CONV_A, W_A), lambda b, j: (l, 0, 0)),
            row(W_A), row(W_A), row(W_A), row(W_B), row(W_B),
            pl.BlockSpec((None, HEADS_B, CHUNK, CHUNK), lambda b, j: (l, 0, 0, 0)),
            pl.BlockSpec((None, CHUNK, W_B), lambda b, j: (l, 0, 0)),
            pl.BlockSpec((None, CONV_C, W_C), lambda b, j: (l, 0, 0)),
        ],
        out_specs=[
            pl.BlockSpec((TILE_T, D_MODEL), lambda b, j: (b * nt + j, 0)),
            pl.BlockSpec((None, CONV_A - 1, W_A), lambda b, j: (b, 0, 0)),
            pl.BlockSpec((None, CONV_C - 1, W_C), lambda b, j: (b, 0, 0)),
        ],
        out_shape=[
            jax.ShapeDtypeStruct((BATCH * SEQ, D_MODEL), F32),
            jax.ShapeDtypeStruct((BATCH, CONV_A - 1, W_A), F32),
            jax.ShapeDtypeStruct((BATCH, CONV_C - 1, W_C), F32),
        ],
        scratch_shapes=[
            pltpu.VMEM((D_MODEL, P_IN), BF16),
            pltpu.VMEM((D_MIX, D_MODEL), BF16),
            pltpu.SemaphoreType.DMA((2,)),
            pltpu.VMEM((TILE_T, D_MODEL), BF16),
            pltpu.VMEM((TILE_T, P_IN), F32),
            pltpu.VMEM((N_LANE_CHUNKS, TILE_T + HIST_A, LANES), F32),
            pltpu.VMEM((N_LANE_CHUNKS, TILE_T + HIST_C, LANES), F32),
            pltpu.VMEM((TILE_T, W_A), F32),
            pltpu.VMEM((TILE_T, W_B), BF16),
            pltpu.VMEM((TILE_T, D_MIX), BF16),
        ],
        compiler_params=pltpu.CompilerParams(
            dimension_semantics=("arbitrary", "arbitrary"), vmem_limit_bytes=VMEM_LIMIT),
        name=f"prompt_layer{l}",
    )(x, mod4, g_pre, g_post, w_in, w_out, a_conv_w, a_conv_b,
      a_ln_g, a_ln_b, b_ln_g, b_ln_b, b_ws, bsb, c_conv_w)


def _sample_kernel(x_ref, mod_ref, gpre_ref, gpost_ref, win_ref, wout_ref,
                   acw_ref, acb_ref, alng_ref, alnb_ref, blng_ref, blnb_ref,
                   wss_ref, bss_ref, ccw_ref, ha_ref, hc_ref,
                   y_ref, ca_ref, cc_ref, v_ref,
                   mix_ref):
    S, N = DEC_SEQ, SAMPLE_N
    l = pl.program_id(1)

    @pl.when(l == 0)
    def _():
        y_ref[...] = x_ref[...]

    shift = mod_ref[:, 0:D_MODEL]
    scale = mod_ref[:, D_MODEL:2 * D_MODEL]
    gate = mod_ref[:, 2 * D_MODEL:3 * D_MODEL]
    pre = gpre_ref[...] * (1.0 + scale)
    xs = [y_ref[t] for t in range(S)]
    hs = []
    for t in range(S):
        r = lax.rsqrt(jnp.mean(xs[t] * xs[t], axis=-1, keepdims=True) + RMS_EPS)
        hs.append(((xs[t] * r) * pre + shift).astype(BF16))
    hb = jnp.concatenate(hs, axis=0)

    pa = jnp.dot(hb, win_ref[:, COL_A:COL_A + 3 * W_A], preferred_element_type=F32)
    a = pa[:, 0:W_A] * jax.nn.sigmoid(pa[:, W_A:2 * W_A])
    a_z = pa[:, 2 * W_A:3 * W_A]
    n_hist = CONV_A - 1
    a_ext = lambda i: ha_ref[i] if i < n_hist else a[(i - n_hist) * N:(i - n_hist + 1) * N]
    for k in range(n_hist):
        ca_ref[k] = a_ext(k + S)
    for t in range(S):
        acc = jnp.broadcast_to(acb_ref[...], (N, W_A))
        for k in range(CONV_A):
            acc = acc + a_ext(t + k) * acw_ref[k:k + 1, :]
        ln = _layernorm(acc, alng_ref[...], alnb_ref[...])
        y_a = _silu(ln) * _silu(a_z[t * N:(t + 1) * N])
        mix_ref[t * N:(t + 1) * N, 0:W_A] = y_a.astype(BF16)

    pb = jnp.dot(hb, win_ref[:, COL_B:COL_B + 3 * W_B], preferred_element_type=F32)
    v_n = _layernorm(pb[:, W_B:2 * W_B], blng_ref[...], blnb_ref[...])
    vs = [v_n[t * N:(t + 1) * N] for t in range(S)]
    for t in range(S):
        v_ref[t] = vs[t]
        s = jnp.broadcast_to(bss_ref[t:t + 1, :], (N, W_B))
        for u in range(t + 1):
            s = s + vs[u] * wss_ref[t, u:u + 1, :]
        rows = slice(t * N, (t + 1) * N)
        y_b = pb[rows, 0:W_B] * s * _silu(pb[rows, 2 * W_B:3 * W_B])
        mix_ref[rows, W_A:W_A + W_B] = y_b.astype(BF16)

    pc = jnp.dot(hb, win_ref[:, COL_C:COL_C + 4 * W_C], preferred_element_type=F32)
    cx = pc[:, W_C:2 * W_C] * pc[:, 2 * W_C:3 * W_C]
    n_hc = CONV_C - 1
    c_ext = lambda i: hc_ref[i] if i < n_hc else cx[(i - n_hc) * N:(i - n_hc + 1) * N]
    for k in range(n_hc):
        cc_ref[k] = c_ext(k + S)
    for t in range(S):
        acc = c_ext(t) * ccw_ref[0:1, :]
        for k in range(1, CONV_C):
            acc = acc + c_ext(t + k) * ccw_ref[k:k + 1, :]
        rows = slice(t * N, (t + 1) * N)
        y_c = pc[rows, 0:W_C] * acc * _silu(pc[rows, 3 * W_C:4 * W_C])
        mix_ref[rows, W_A + W_B:D_MIX] = y_c.astype(BF16)

    mix = jnp.dot(mix_ref[...], wout_ref[...], preferred_element_type=F32)
    r2 = lax.rsqrt(jnp.mean(mix * mix, axis=-1, keepdims=True) + RMS_EPS)
    post = gate * gpost_ref[...]
    for t in range(S):
        rows = slice(t * N, (t + 1) * N)
        y_ref[t] = xs[t] + (mix[rows] * r2[rows]) * post


def _sample_layers(x_t, mod_all, g_pre, g_post, w_in, w_out, a_conv_w, a_conv_b,
                   a_ln_g, a_ln_b, b_ln_g, b_ln_b, wss, bss, c_conv_w, ha_t, hc_t):
    S, N = DEC_SEQ, SAMPLE_N
    nsplit = DEC_BATCH // N
    row = lambda w: pl.BlockSpec((None, 1, w), lambda n, l: (l, 0, 0))
    lay3 = lambda a, b: pl.BlockSpec((None, a, b), lambda n, l: (l, 0, 0))
    lay4 = lambda a, b, c: pl.BlockSpec((None, a, b, c), lambda n, l: (l, 0, 0, 0))
    seq4 = lambda a, c: pl.BlockSpec((None, a, N, c), lambda n, l: (l, 0, n, 0))
    return pl.pallas_call(
        _sample_kernel,
        grid=(nsplit, DEPTH),
        in_specs=[
            pl.BlockSpec((S, N, D_MODEL), lambda n, l: (0, n, 0)),
            pl.BlockSpec((None, N, 3 * D_MODEL), lambda n, l: (l, n, 0)),
            row(D_MODEL), row(D_MODEL),
            lay3(D_MODEL, P_IN), lay3(D_MIX, D_MODEL),
            lay3(CONV_A, W_A),
            row(W_A), row(W_A), row(W_A), row(W_B), row(W_B),
            lay4(S, S, W_B), lay3(S, W_B), lay3(CONV_C, W_C),
            seq4(CONV_A - 1, W_A), seq4(CONV_C - 1, W_C),
        ],
        out_specs=[
            pl.BlockSpec((S, N, D_MODEL), lambda n, l: (0, n, 0)),
            seq4(CONV_A - 1, W_A), seq4(CONV_C - 1, W_C), seq4(S, W_B),
        ],
        out_shape=[
            jax.ShapeDtypeStruct((S, DEC_BATCH, D_MODEL), F32),
            jax.ShapeDtypeStruct((DEPTH, CONV_A - 1, DEC_BATCH, W_A), F32),
            jax.ShapeDtypeStruct((DEPTH, CONV_C - 1, DEC_BATCH, W_C), F32),
            jax.ShapeDtypeStruct((DEPTH, S, DEC_BATCH, W_B), F32),
        ],
        scratch_shapes=[pltpu.VMEM((S * N, D_MIX), BF16)],
        compiler_params=pltpu.CompilerParams(
            dimension_semantics=("arbitrary", "arbitrary"), vmem_limit_bytes=VMEM_LIMIT),
        name="sample_layers",
    )(x_t, mod_all, g_pre, g_post, w_in, w_out, a_conv_w, a_conv_b,
      a_ln_g, a_ln_b, b_ln_g, b_ln_b, wss, bss, c_conv_w, ha_t, hc_t)


def kernel(x_prompt, x_sample, c_prompt, c_sample, state_conv_a, state_conv_c, w_mod, b_mod,
           g_pre, g_post, w_in, w_out, a_conv_w, a_conv_b, a_ln_g, a_ln_b, b_ln_g, b_ln_b,
           b_ws, b_bs, c_conv_w):
    S = DEC_SEQ
    w_in_b = w_in.astype(BF16)
    w_out_b = w_out.astype(BF16)
    r3 = lambda p: p.reshape(DEPTH, 1, p.shape[-1])
    g_pre3, g_post3 = r3(g_pre), r3(g_post)
    acb3, alng3, alnb3, blng3, blnb3 = r3(a_conv_b), r3(a_ln_g), r3(a_ln_b), r3(b_ln_g), r3(b_ln_b)

    c_all = jnp.concatenate([c_sample, c_prompt], axis=0)
    mod_all = _mod_call(c_all, w_mod, b_mod)
    mod4 = mod_all.reshape(DEPTH, N_ALL, 3, D_MODEL)

    bsb = jnp.repeat(jnp.swapaxes(b_bs, 1, 2), HEAD_W, axis=2)
    wss = jnp.repeat(jnp.transpose(b_ws[:, :, :S, :S], (0, 2, 3, 1)), HEAD_W, axis=3)
    bss = bsb[:, :S, :]

    xp = x_prompt.reshape(BATCH * SEQ, D_MODEL)
    ca_p, cc_p = [], []
    for l in range(DEPTH):
        xp, ca, cc = _prompt_layer(l, xp, mod4, g_pre3, g_post3, w_in_b, w_out_b, a_conv_w, acb3,
                                   alng3, alnb3, blng3, blnb3, b_ws, bsb, c_conv_w)
        ca_p.append(ca)
        cc_p.append(cc)

    x_t = jnp.swapaxes(x_sample, 0, 1)
    ha_t = jnp.swapaxes(state_conv_a, 1, 2)
    hc_t = jnp.swapaxes(state_conv_c, 1, 2)
    y_t, ca_t, cc_t, v_t = _sample_layers(
        x_t, mod_all, g_pre3, g_post3, w_in_b, w_out_b, a_conv_w, acb3,
        alng3, alnb3, blng3, blnb3, wss, bss, c_conv_w, ha_t, hc_t)

    return (xp.reshape(BATCH, SEQ, D_MODEL),
            jnp.swapaxes(y_t, 0, 1),
            jnp.stack(ca_p), jnp.stack(cc_p),
            jnp.swapaxes(ca_t, 1, 2), jnp.swapaxes(cc_t, 1, 2), jnp.swapaxes(v_t, 1, 2))
```

```python
import functools

import jax
import jax.numpy as jnp
from jax import lax
from jax.experimental import pallas as pl
from jax.experimental.pallas import tpu as pltpu

D_MODEL = 1024
BATCH = 8
SEQ = 2048
DEPTH = 4
DEC_BATCH = 128
DEC_SEQ = 4
W_A = 512
W_B = 512
W_C = 512
D_MIX = W_A + W_B + W_C
P_IN = 3 * W_A + 3 * W_B + 4 * W_C
CONV_A = 31
CONV_C = 3
CHUNK = 128
HEADS_B = 4
HEAD_W = W_B // HEADS_B
RMS_EPS = 1e-6
LN_EPS = 1e-5

LANES = 128
SUBLANES = 8
N_LANE_CHUNKS = W_A // LANES

COL_A = 0
COL_B = 3 * W_A
COL_C = 3 * W_A + 3 * W_B

TILE_T = 512
HIST_A = 32
HIST_C = 8
CONV_ROWS = 16
EDGE_ROWS = TILE_T // 4
N_ALL = DEC_BATCH + BATCH
SAMPLE_N = 64

VMEM_LIMIT = 56 * 1024 * 1024

F32 = jnp.float32
BF16 = jnp.bfloat16


def _silu(x):
    return x * jax.nn.sigmoid(x)


def _layernorm(x, g, b):
    mu = jnp.mean(x, axis=-1, keepdims=True)
    xc = x - mu
    var = jnp.mean(xc * xc, axis=-1, keepdims=True)
    return xc * lax.rsqrt(var + LN_EPS) * g + b


def _mod_kernel(c_ref, w_ref, b_ref, o_ref):
    c = c_ref[...]
    s = _silu(c).astype(BF16)
    o_ref[...] = jnp.dot(s, w_ref[...].astype(BF16), preferred_element_type=F32) + b_ref[...]


def _mod_call(c_all, w_mod, b_mod):
    tn = D_MODEL
    return pl.pallas_call(
        _mod_kernel,
        grid=(DEPTH, 3 * D_MODEL // tn),
        in_specs=[
            pl.BlockSpec((N_ALL, D_MODEL), lambda l, n: (0, 0)),
            pl.BlockSpec((None, D_MODEL, tn), lambda l, n: (l, 0, n)),
            pl.BlockSpec((None, 1, tn), lambda l, n: (l, 0, n)),
        ],
        out_specs=pl.BlockSpec((None, N_ALL, tn), lambda l, n: (l, 0, n)),
        out_shape=jax.ShapeDtypeStruct((DEPTH, N_ALL, 3 * D_MODEL), F32),
        compiler_params=pltpu.CompilerParams(
            dimension_semantics=("arbitrary", "arbitrary"), vmem_limit_bytes=VMEM_LIMIT),
        name="mod",
    )(c_all, w_mod, b_mod.reshape(DEPTH, 1, 3 * D_MODEL))


def _prompt_kernel(cast_next, x_ref, mod_ref, gpre_ref, gpost_ref, win_hbm, wout_hbm,
                   acw_ref, acb_ref, alng_ref, alnb_ref, blng_ref, blnb_ref,
                   ws_ref, bsb_ref, ccw_ref, *rest):
    if cast_next:
        nwin_f32, nwout_f32, y_ref, ca_ref, cc_ref, nwin_b, nwout_b, *scratch = rest
        nwin_b[...] = nwin_f32[...].astype(BF16)
        nwout_b[...] = nwout_f32[...].astype(BF16)
    else:
        y_ref, ca_ref, cc_ref, *scratch = rest
    win_ref, wout_ref, w_sem, hb_ref, p_ref, aext_ref, cext_ref, aconv_ref, vb_ref, mix_ref = scratch
    T = TILE_T
    j = pl.program_id(1)
    off_a = HIST_A - (CONV_A - 1)
    off_c = HIST_C - (CONV_C - 1)

    @pl.when((pl.program_id(0) == 0) & (j == 0))
    def _():
        cp_in = pltpu.make_async_copy(win_hbm, win_ref, w_sem.at[0])
        cp_out = pltpu.make_async_copy(wout_hbm, wout_ref, w_sem.at[1])
        cp_in.start()
        cp_out.start()
        cp_in.wait()
        cp_out.wait()

    @pl.when(j == 0)
    def _():
        aext_ref[:, 0:HIST_A, :] = jnp.zeros((N_LANE_CHUNKS, HIST_A, LANES), F32)
        cext_ref[:, 0:HIST_C, :] = jnp.zeros((N_LANE_CHUNKS, HIST_C, LANES), F32)

    shift = mod_ref[0:1, :]
    scale = mod_ref[1:2, :]
    gate = mod_ref[2:3, :]
    pre = gpre_ref[...] * (1.0 + scale)
    for h0 in range(0, T, EDGE_ROWS):
        hr = slice(h0, h0 + EDGE_ROWS)
        x = x_ref[hr, :]
        r = lax.rsqrt(jnp.mean(x * x, axis=-1, keepdims=True) + RMS_EPS)
        hb_ref[hr, :] = ((x * r) * pre + shift).astype(BF16)

    def proj(col):
        p_ref[:, col:col + W_A] = jnp.dot(hb_ref[...], win_ref[:, col:col + W_A], preferred_element_type=F32)

    def pcol(col):
        return p_ref[:, col:col + W_A]

    def conv_a_chunk(c):
        lanes = slice(c * LANES, (c + 1) * LANES)
        for r0 in range(0, T, CONV_ROWS):
            acc = jnp.broadcast_to(acb_ref[:, lanes], (CONV_ROWS, LANES))
            for k in range(CONV_A):
                s0 = r0 + off_a + k
                acc = acc + aext_ref[c, s0:s0 + CONV_ROWS, :] * acw_ref[k:k + 1, lanes]
            aconv_ref[r0:r0 + CONV_ROWS, lanes] = acc

    for h0 in range(0, T, EDGE_ROWS):
        hr = slice(h0, h0 + EDGE_ROWS)
        for col in (COL_A, COL_A + W_A):
            p_ref[hr, col:col + W_A] = jnp.dot(hb_ref[hr, :], win_ref[:, col:col + W_A], preferred_element_type=F32)
    a = pcol(COL_A) * jax.nn.sigmoid(pcol(COL_A + W_A))
    for c in range(N_LANE_CHUNKS):
        aext_ref[c, HIST_A:HIST_A + T, :] = a[:, c * LANES:(c + 1) * LANES]
    proj(COL_A + 2 * W_A)
    conv_a_chunk(0)

    proj(COL_B + W_B)
    proj(COL_B)
    proj(COL_B + 2 * W_B)
    vb_ref[...] = _layernorm(pcol(COL_B + W_B), blng_ref[...], blnb_ref[...]).astype(BF16)
    conv_a_chunk(1)
    row_i = lax.broadcasted_iota(jnp.int32, (CHUNK, CHUNK), 0)
    col_i = lax.broadcasted_iota(jnp.int32, (CHUNK, CHUNK), 1)
    wm = [jnp.where(row_i >= col_i, ws_ref[hd], 0.0).astype(BF16) for hd in range(HEADS_B)]
    col_s = COL_B + W_B
    for ch in range(T // CHUNK):
        rows = slice(ch * CHUNK, (ch + 1) * CHUNK)
        for hd in range(HEADS_B):
            hcols = slice(hd * HEAD_W, (hd + 1) * HEAD_W)
            p_ref[rows, col_s + hd * HEAD_W:col_s + (hd + 1) * HEAD_W] = (
                jnp.dot(wm[hd], vb_ref[rows, hcols], preferred_element_type=F32) + bsb_ref[:, hcols])
    y_b = pcol(COL_B) * pcol(col_s) * _silu(pcol(COL_B + 2 * W_B))
    mix_ref[:, W_A:W_A + W_B] = y_b.astype(BF16)

    proj(COL_C + W_C)
    proj(COL_C + 2 * W_C)
    conv_a_chunk(2)
    cx = pcol(COL_C + W_C) * pcol(COL_C + 2 * W_C)
    for c in range(N_LANE_CHUNKS):
        cext_ref[c, HIST_C:HIST_C + T, :] = cx[:, c * LANES:(c + 1) * LANES]
    proj(COL_C)
    proj(COL_C + 3 * W_C)
    conv_a_chunk(3)
    for c in range(N_LANE_CHUNKS):
        lanes = slice(c * LANES, (c + 1) * LANES)
        acc = cext_ref[c, off_c:off_c + T, :] * ccw_ref[0:1, lanes]
        for k in range(1, CONV_C):
            acc = acc + cext_ref[c, off_c + k:off_c + k + T, :] * ccw_ref[k:k + 1, lanes]
        y_c = p_ref[:, COL_C + c * LANES:COL_C + (c + 1) * LANES] * acc * _silu(
            p_ref[:, COL_C + 3 * W_C + c * LANES:COL_C + 3 * W_C + (c + 1) * LANES])
        mix_ref[:, W_A + W_B + c * LANES:W_A + W_B + (c + 1) * LANES] = y_c.astype(BF16)

    out_bc = jnp.dot(mix_ref[:, W_A:D_MIX], wout_ref[W_A:D_MIX, :], preferred_element_type=F32)
    y_a = _silu(_layernorm(aconv_ref[...], alng_ref[...], alnb_ref[...])) * _silu(pcol(COL_A + 2 * W_A))
    mix_ref[:, 0:W_A] = y_a.astype(BF16)
    post = gate * gpost_ref[...]
    for h0 in range(0, T, EDGE_ROWS):
        hr = slice(h0, h0 + EDGE_ROWS)
        mix = out_bc[hr] + jnp.dot(mix_ref[hr, 0:W_A], wout_ref[0:W_A, :], preferred_element_type=F32)
        r2 = lax.rsqrt(jnp.mean(mix * mix, axis=-1, keepdims=True) + RMS_EPS)
        y_ref[hr, :] = x_ref[hr, :] + (mix * r2) * post

    @pl.when(j == pl.num_programs(1) - 1)
    def _():
        for c in range(N_LANE_CHUNKS):
            ca_ref[:, c * LANES:(c + 1) * LANES] = aext_ref[c, T + off_a:T + HIST_A, :]
            cc_ref[:, c * LANES:(c + 1) * LANES] = cext_ref[c, T + off_c:T + HIST_C, :]

    for c in range(N_LANE_CHUNKS):
        aext_ref[c, 0:HIST_A, :] = aext_ref[c, T:T + HIST_A, :]
        cext_ref[c, 0:HIST_C, :] = cext_ref[c, T:T + HIST_C, :]


def _prompt_layer(l, x, mod4, g_pre, g_post, w_in_b, w_out_b, w_in, w_out, a_conv_w, a_conv_b,
                  a_ln_g, a_ln_b, b_ln_g, b_ln_b, b_ws, bsb, c_conv_w):
    nt = SEQ // TILE_T
    n_steps = BATCH * nt
    cast_next = l + 1 < DEPTH
    rows_in, rows_out = D_MODEL // n_steps, D_MIX // n_steps
    assert rows_in * n_steps == D_MODEL and rows_out * n_steps == D_MIX
    assert rows_in % (2 * SUBLANES) == 0 and rows_out % (2 * SUBLANES) == 0
    row = lambda w: pl.BlockSpec((None, 1, w), lambda b, j: (l, 0, 0))
    in_specs = [
        pl.BlockSpec((TILE_T, D_MODEL), lambda b, j: (b * nt + j, 0)),
        pl.BlockSpec((None, None, 3, D_MODEL), lambda b, j: (l, DEC_BATCH + b, 0, 0)),
        row(D_MODEL), row(D_MODEL),
        pl.BlockSpec(memory_space=pl.ANY),
        pl.BlockSpec(memory_space=pl.ANY),
        pl.BlockSpec((None, CONV_A, W_A), lambda b, j: (l, 0, 0)),
        row(W_A), row(W_A), row(W_A), row(W_B), row(W_B),
        pl.BlockSpec((None, HEADS_B, CHUNK, CHUNK), lambda b, j: (l, 0, 0, 0)),
        pl.BlockSpec((None, CHUNK, W_B), lambda b, j: (l, 0, 0)),
        pl.BlockSpec((None, CONV_C, W_C), lambda b, j: (l, 0, 0)),
    ]
    out_specs = [
        pl.BlockSpec((TILE_T, D_MODEL), lambda b, j: (b * nt + j, 0)),
        pl.BlockSpec((None, CONV_A - 1, W_A), lambda b, j: (b, 0, 0)),
        pl.BlockSpec((None, CONV_C - 1, W_C), lambda b, j: (b, 0, 0)),
    ]
    out_shape = [
        jax.ShapeDtypeStruct((BATCH * SEQ, D_MODEL), F32),
        jax.ShapeDtypeStruct((BATCH, CONV_A - 1, W_A), F32),
        jax.ShapeDtypeStruct((BATCH, CONV_C - 1, W_C), F32),
    ]
    args = [x, mod4, g_pre, g_post, w_in_b, w_out_b, a_conv_w, a_conv_b,
            a_ln_g, a_ln_b, b_ln_g, b_ln_b, b_ws, bsb, c_conv_w]
    if cast_next:
        in_specs += [
            pl.BlockSpec((None, rows_in, P_IN), lambda b, j: (l + 1, b * nt + j, 0)),
            pl.BlockSpec((None, rows_out, D_MODEL), lambda b, j: (l + 1, b * nt + j, 0)),
        ]
        out_specs += [
            pl.BlockSpec((rows_in, P_IN), lambda b, j: (b * nt + j, 0)),
            pl.BlockSpec((rows_out, D_MODEL), lambda b, j: (b * nt + j, 0)),
        ]
        out_shape += [
            jax.ShapeDtypeStruct((D_MODEL, P_IN), BF16),
            jax.ShapeDtypeStruct((D_MIX, D_MODEL), BF16),
        ]
        args += [w_in, w_out]
    return pl.pallas_call(
        functools.partial(_prompt_kernel, cast_next),
        grid=(BATCH, nt),
        in_specs=in_specs,
        out_specs=out_specs,
        out_shape=out_shape,
        scratch_shapes=[
            pltpu.VMEM((D_MODEL, P_IN), BF16),
            pltpu.VMEM((D_MIX, D_MODEL), BF16),
            pltpu.SemaphoreType.DMA((2,)),
            pltpu.VMEM((TILE_T, D_MODEL), BF16),
            pltpu.VMEM((TILE_T, P_IN), F32),
            pltpu.VMEM((N_LANE_CHUNKS, TILE_T + HIST_A, LANES), F32),
            pltpu.VMEM((N_LANE_CHUNKS, TILE_T + HIST_C, LANES), F32),
            pltpu.VMEM((TILE_T, W_A), F32),
            pltpu.VMEM((TILE_T, W_B), BF16),
            pltpu.VMEM((TILE_T, D_MIX), BF16),
        ],
        compiler_params=pltpu.CompilerParams(
            dimension_semantics=("arbitrary", "arbitrary"), vmem_limit_bytes=VMEM_LIMIT),
        name=f"prompt_layer{l}",
    )(*args)


def _sample_kernel(x_ref, mod_ref, gpre_ref, gpost_ref, win_ref, wout_ref,
                   acw_ref, acb_ref, alng_ref, alnb_ref, blng_ref, blnb_ref,
                   wss_ref, bss_ref, ccw_ref, ha_ref, hc_ref,
                   y_ref, ca_ref, cc_ref, v_ref,
                   mix_ref):
    S, N = DEC_SEQ, SAMPLE_N
    shift = mod_ref[:, 0:D_MODEL]
    scale = mod_ref[:, D_MODEL:2 * D_MODEL]
    gate = mod_ref[:, 2 * D_MODEL:3 * D_MODEL]
    pre = gpre_ref[...] * (1.0 + scale)
    xs = [x_ref[t] for t in range(S)]
    hs = []
    for t in range(S):
        r = lax.rsqrt(jnp.mean(xs[t] * xs[t], axis=-1, keepdims=True) + RMS_EPS)
        hs.append(((xs[t] * r) * pre + shift).astype(BF16))
    hb = jnp.concatenate(hs, axis=0)

    pa = jnp.dot(hb, win_ref[:, COL_A:COL_A + 3 * W_A], preferred_element_type=F32)
    a = pa[:, 0:W_A] * jax.nn.sigmoid(pa[:, W_A:2 * W_A])
    a_z = pa[:, 2 * W_A:3 * W_A]
    n_hist = CONV_A - 1
    a_ext = lambda i: ha_ref[i] if i < n_hist else a[(i - n_hist) * N:(i - n_hist + 1) * N]
    for k in range(n_hist):
        ca_ref[k] = a_ext(k + S)
    for t in range(S):
        acc = jnp.broadcast_to(acb_ref[...], (N, W_A))
        for k in range(CONV_A):
            acc = acc + a_ext(t + k) * acw_ref[k:k + 1, :]
        ln = _layernorm(acc, alng_ref[...], alnb_ref[...])
        y_a = _silu(ln) * _silu(a_z[t * N:(t + 1) * N])
        mix_ref[t * N:(t + 1) * N, 0:W_A] = y_a.astype(BF16)

    pb = jnp.dot(hb, win_ref[:, COL_B:COL_B + 3 * W_B], preferred_element_type=F32)
    v_n = _layernorm(pb[:, W_B:2 * W_B], blng_ref[...], blnb_ref[...])
    vs = [v_n[t * N:(t + 1) * N] for t in range(S)]
    for t in range(S):
        v_ref[t] = vs[t]
        s = jnp.broadcast_to(bss_ref[t:t + 1, :], (N, W_B))
        for u in range(t + 1):
            s = s + vs[u] * wss_ref[t, u:u + 1, :]
        rows = slice(t * N, (t + 1) * N)
        y_b = pb[rows, 0:W_B] * s * _silu(pb[rows, 2 * W_B:3 * W_B])
        mix_ref[rows, W_A:W_A + W_B] = y_b.astype(BF16)

    pc = jnp.dot(hb, win_ref[:, COL_C:COL_C + 4 * W_C], preferred_element_type=F32)
    cx = pc[:, W_C:2 * W_C] * pc[:, 2 * W_C:3 * W_C]
    n_hc = CONV_C - 1
    c_ext = lambda i: hc_ref[i] if i < n_hc else cx[(i - n_hc) * N:(i - n_hc + 1) * N]
    for k in range(n_hc):
        cc_ref[k] = c_ext(k + S)
    for t in range(S):
        acc = c_ext(t) * ccw_ref[0:1, :]
        for k in range(1, CONV_C):
            acc = acc + c_ext(t + k) * ccw_ref[k:k + 1, :]
        rows = slice(t * N, (t + 1) * N)
        y_c = pc[rows, 0:W_C] * acc * _silu(pc[rows, 3 * W_C:4 * W_C])
        mix_ref[rows, W_A + W_B:D_MIX] = y_c.astype(BF16)

    mix = jnp.dot(mix_ref[...], wout_ref[...], preferred_element_type=F32)
    r2 = lax.rsqrt(jnp.mean(mix * mix, axis=-1, keepdims=True) + RMS_EPS)
    post = gate * gpost_ref[...]
    for t in range(S):
        rows = slice(t * N, (t + 1) * N)
        y_ref[t] = xs[t] + (mix[rows] * r2[rows]) * post


def _sample_layer(l, x_t, mod_all, g_pre, g_post, w_in_b, w_out_b, a_conv_w, a_conv_b,
                  a_ln_g, a_ln_b, b_ln_g, b_ln_b, wss, bss, c_conv_w, ha_t, hc_t):
    S, N = DEC_SEQ, SAMPLE_N
    nsplit = DEC_BATCH // N
    row = lambda w: pl.BlockSpec((None, 1, w), lambda n: (l, 0, 0))
    lay3 = lambda a, b: pl.BlockSpec((None, a, b), lambda n: (l, 0, 0))
    lay4 = lambda a, b, c: pl.BlockSpec((None, a, b, c), lambda n: (l, 0, 0, 0))
    full = lambda a, b: pl.BlockSpec((a, b), lambda n: (0, 0))
    seq_in = lambda a, c: pl.BlockSpec((None, a, N, c), lambda n: (l, 0, n, 0))
    seq_out = lambda a, c: pl.BlockSpec((a, N, c), lambda n: (0, n, 0))
    return pl.pallas_call(
        _sample_kernel,
        grid=(nsplit,),
        in_specs=[
            pl.BlockSpec((S, N, D_MODEL), lambda n: (0, n, 0)),
            pl.BlockSpec((None, N, 3 * D_MODEL), lambda n: (l, n, 0)),
            row(D_MODEL), row(D_MODEL),
            full(D_MODEL, P_IN), full(D_MIX, D_MODEL),
            lay3(CONV_A, W_A),
            row(W_A), row(W_A), row(W_A), row(W_B), row(W_B),
            lay4(S, S, W_B), lay3(S, W_B), lay3(CONV_C, W_C),
            seq_in(CONV_A - 1, W_A), seq_in(CONV_C - 1, W_C),
        ],
        out_specs=[
            pl.BlockSpec((S, N, D_MODEL), lambda n: (0, n, 0)),
            seq_out(CONV_A - 1, W_A), seq_out(CONV_C - 1, W_C), seq_out(S, W_B),
        ],
        out_shape=[
            jax.ShapeDtypeStruct((S, DEC_BATCH, D_MODEL), F32),
            jax.ShapeDtypeStruct((CONV_A - 1, DEC_BATCH, W_A), F32),
            jax.ShapeDtypeStruct((CONV_C - 1, DEC_BATCH, W_C), F32),
            jax.ShapeDtypeStruct((S, DEC_BATCH, W_B), F32),
        ],
        scratch_shapes=[pltpu.VMEM((S * N, D_MIX), BF16)],
        compiler_params=pltpu.CompilerParams(
            dimension_semantics=("arbitrary",), vmem_limit_bytes=VMEM_LIMIT),
        name=f"sample_layer{l}",
    )(x_t, mod_all, g_pre, g_post, w_in_b, w_out_b, a_conv_w, a_conv_b,
      a_ln_g, a_ln_b, b_ln_g, b_ln_b, wss, bss, c_conv_w, ha_t, hc_t)


def kernel(x_prompt, x_sample, c_prompt, c_sample, state_conv_a, state_conv_c, w_mod, b_mod,
           g_pre, g_post, w_in, w_out, a_conv_w, a_conv_b, a_ln_g, a_ln_b, b_ln_g, b_ln_b,
           b_ws, b_bs, c_conv_w):
    S = DEC_SEQ
    w_in_b = [w_in[0].astype(BF16)]
    w_out_b = [w_out[0].astype(BF16)]
    r3 = lambda p: p.reshape(DEPTH, 1, p.shape[-1])
    g_pre3, g_post3 = r3(g_pre), r3(g_post)
    acb3, alng3, alnb3, blng3, blnb3 = r3(a_conv_b), r3(a_ln_g), r3(a_ln_b), r3(b_ln_g), r3(b_ln_b)

    c_all = jnp.concatenate([c_sample, c_prompt], axis=0)
    mod_all = _mod_call(c_all, w_mod, b_mod)
    mod4 = mod_all.reshape(DEPTH, N_ALL, 3, D_MODEL)

    bsb = jnp.repeat(jnp.swapaxes(b_bs, 1, 2), HEAD_W, axis=2)
    wss = jnp.repeat(jnp.transpose(b_ws[:, :, :S, :S], (0, 2, 3, 1)), HEAD_W, axis=3)
    bss = bsb[:, :S, :]

    xp = x_prompt.reshape(BATCH * SEQ, D_MODEL)
    ca_p, cc_p = [], []
    for l in range(DEPTH):
        outs = _prompt_layer(l, xp, mod4, g_pre3, g_post3, w_in_b[l], w_out_b[l], w_in, w_out, a_conv_w, acb3,
                             alng3, alnb3, blng3, blnb3, b_ws, bsb, c_conv_w)
        xp, ca, cc = outs[:3]
        ca_p.append(ca)
        cc_p.append(cc)
        if l + 1 < DEPTH:
            w_in_b.append(outs[3])
            w_out_b.append(outs[4])

    x_t = jnp.swapaxes(x_sample, 0, 1)
    ha_t = jnp.swapaxes(state_conv_a, 1, 2)
    hc_t = jnp.swapaxes(state_conv_c, 1, 2)
    y_t = x_t
    ca_s, cc_s, v_s = [], [], []
    for l in range(DEPTH):
        y_t, ca_t, cc_t, v_t = _sample_layer(
            l, y_t, mod_all, g_pre3, g_post3, w_in_b[l], w_out_b[l], a_conv_w, acb3,
            alng3, alnb3, blng3, blnb3, wss, bss, c_conv_w, ha_t, hc_t)
        ca_s.append(ca_t)
        cc_s.append(cc_t)
        v_s.append(v_t)

    return (xp.reshape(BATCH, SEQ, D_MODEL),
            jnp.swapaxes(y_t, 0, 1),
            jnp.stack(ca_p), jnp.stack(cc_p),
            jnp.swapaxes(jnp.stack(ca_s), 1, 2), jnp.swapaxes(jnp.stack(cc_s), 1, 2),
            jnp.swapaxes(jnp.stack(v_s), 1, 2))
```

```python
import functools

import jax
import jax.numpy as jnp
from jax import lax
from jax.experimental import pallas as pl
from jax.experimental.pallas import tpu as pltpu

D_MODEL = 1024
BATCH = 8
SEQ = 2048
DEPTH = 4
DEC_BATCH = 128
DEC_SEQ = 4
W_A = 512
W_B = 512
W_C = 512
D_MIX = W_A + W_B + W_C
P_IN = 3 * W_A + 3 * W_B + 4 * W_C
CONV_A = 31
CONV_C = 3
CHUNK = 128
HEADS_B = 4
HEAD_W = W_B // HEADS_B
RMS_EPS = 1e-6
LN_EPS = 1e-5

LANES = 128
SUBLANES = 8
N_LANE_CHUNKS = W_A // LANES

COL_A = 0
COL_B = 3 * W_A
COL_C = 3 * W_A + 3 * W_B

TILE_T = 512
HIST_A = 32
HIST_C = 8
CONV_ROWS = 16
EDGE_ROWS = TILE_T // 4
N_ALL = DEC_BATCH + BATCH
SAMPLE_N = 64

VMEM_LIMIT = 56 * 1024 * 1024

F32 = jnp.float32
BF16 = jnp.bfloat16


def _silu(x):
    return x * jax.nn.sigmoid(x)


def _layernorm(x, g, b):
    mu = jnp.mean(x, axis=-1, keepdims=True)
    xc = x - mu
    var = jnp.mean(xc * xc, axis=-1, keepdims=True)
    return xc * lax.rsqrt(var + LN_EPS) * g + b


def _mod_kernel(c_ref, w_ref, b_ref, o_ref):
    c = c_ref[...]
    s = _silu(c).astype(BF16)
    o_ref[...] = jnp.dot(s, w_ref[...].astype(BF16), preferred_element_type=F32) + b_ref[...]


def _mod_call(c_all, w_mod, b_mod):
    tn = D_MODEL
    return pl.pallas_call(
        _mod_kernel,
        grid=(DEPTH, 3 * D_MODEL // tn),
        in_specs=[
            pl.BlockSpec((N_ALL, D_MODEL), lambda l, n: (0, 0)),
            pl.BlockSpec((None, D_MODEL, tn), lambda l, n: (l, 0, n)),
            pl.BlockSpec((None, 1, tn), lambda l, n: (l, 0, n)),
        ],
        out_specs=pl.BlockSpec((None, N_ALL, tn), lambda l, n: (l, 0, n)),
        out_shape=jax.ShapeDtypeStruct((DEPTH, N_ALL, 3 * D_MODEL), F32),
        compiler_params=pltpu.CompilerParams(
            dimension_semantics=("arbitrary", "arbitrary"), vmem_limit_bytes=VMEM_LIMIT),
        name="mod",
    )(c_all, w_mod, b_mod.reshape(DEPTH, 1, 3 * D_MODEL))


def _prompt_kernel(cast_next, x_ref, mod_ref, gpre_ref, gpost_ref, win_hbm, wout_hbm,
                   acw_ref, acb_ref, alng_ref, alnb_ref, blng_ref, blnb_ref,
                   ws_ref, bsb_ref, ccw_ref, *rest):
    if cast_next:
        nwin_f32, nwout_f32, y_ref, ca_ref, cc_ref, nwin_b, nwout_b, *scratch = rest
        nwin_b[...] = nwin_f32[...].astype(BF16)
        nwout_b[...] = nwout_f32[...].astype(BF16)
    else:
        y_ref, ca_ref, cc_ref, *scratch = rest
    win_ref, wout_ref, w_sem, hb_ref, p_ref, aext_ref, cext_ref, aconv_ref, vb_ref, mix_ref = scratch
    T = TILE_T
    j = pl.program_id(1)
    off_a = HIST_A - (CONV_A - 1)
    off_c = HIST_C - (CONV_C - 1)

    @pl.when((pl.program_id(0) == 0) & (j == 0))
    def _():
        cp_in = pltpu.make_async_copy(win_hbm, win_ref, w_sem.at[0])
        cp_out = pltpu.make_async_copy(wout_hbm, wout_ref, w_sem.at[1])
        cp_in.start()
        cp_out.start()
        cp_in.wait()
        cp_out.wait()

    @pl.when(j == 0)
    def _():
        aext_ref[:, 0:HIST_A, :] = jnp.zeros((N_LANE_CHUNKS, HIST_A, LANES), F32)
        cext_ref[:, 0:HIST_C, :] = jnp.zeros((N_LANE_CHUNKS, HIST_C, LANES), F32)

    shift = mod_ref[0:1, :]
    scale = mod_ref[1:2, :]
    gate = mod_ref[2:3, :]
    pre = gpre_ref[...] * (1.0 + scale)
    for h0 in range(0, T, EDGE_ROWS):
        hr = slice(h0, h0 + EDGE_ROWS)
        x = x_ref[hr, :]
        r = lax.rsqrt(jnp.mean(x * x, axis=-1, keepdims=True) + RMS_EPS)
        hb_ref[hr, :] = ((x * r) * pre + shift).astype(BF16)

    def proj(col):
        p_ref[:, col:col + W_A] = jnp.dot(hb_ref[...], win_ref[:, col:col + W_A], preferred_element_type=F32)

    def pcol(col):
        return p_ref[:, col:col + W_A]

    def conv_a_chunk(c):
        lanes = slice(c * LANES, (c + 1) * LANES)
        for r0 in range(0, T, CONV_ROWS):
            acc = jnp.broadcast_to(acb_ref[:, lanes], (CONV_ROWS, LANES))
            for k in range(CONV_A):
                s0 = r0 + off_a + k
                acc = acc + aext_ref[c, s0:s0 + CONV_ROWS, :] * acw_ref[k:k + 1, lanes]
            aconv_ref[r0:r0 + CONV_ROWS, lanes] = acc

    for h0 in range(0, T, EDGE_ROWS):
        hr = slice(h0, h0 + EDGE_ROWS)
        for col in (COL_A, COL_A + W_A):
            p_ref[hr, col:col + W_A] = jnp.dot(hb_ref[hr, :], win_ref[:, col:col + W_A], preferred_element_type=F32)
    a = pcol(COL_A) * jax.nn.sigmoid(pcol(COL_A + W_A))
    for c in range(N_LANE_CHUNKS):
        aext_ref[c, HIST_A:HIST_A + T, :] = a[:, c * LANES:(c + 1) * LANES]
    proj(COL_A + 2 * W_A)
    conv_a_chunk(0)

    proj(COL_B + W_B)
    proj(COL_B)
    proj(COL_B + 2 * W_B)
    vb_ref[...] = _layernorm(pcol(COL_B + W_B), blng_ref[...], blnb_ref[...]).astype(BF16)
    conv_a_chunk(1)
    row_i = lax.broadcasted_iota(jnp.int32, (CHUNK, CHUNK), 0)
    col_i = lax.broadcasted_iota(jnp.int32, (CHUNK, CHUNK), 1)
    wm = [jnp.where(row_i >= col_i, ws_ref[hd], 0.0).astype(BF16) for hd in range(HEADS_B)]
    col_s = COL_B + W_B
    for ch in range(T // CHUNK):
        rows = slice(ch * CHUNK, (ch + 1) * CHUNK)
        for hd in range(HEADS_B):
            hcols = slice(hd * HEAD_W, (hd + 1) * HEAD_W)
            p_ref[rows, col_s + hd * HEAD_W:col_s + (hd + 1) * HEAD_W] = (
                jnp.dot(wm[hd], vb_ref[rows, hcols], preferred_element_type=F32) + bsb_ref[:, hcols])
    y_b = pcol(COL_B) * pcol(col_s) * _silu(pcol(COL_B + 2 * W_B))
    mix_ref[:, W_A:W_A + W_B] = y_b.astype(BF16)

    proj(COL_C + W_C)
    proj(COL_C + 2 * W_C)
    conv_a_chunk(2)
    cx = pcol(COL_C + W_C) * pcol(COL_C + 2 * W_C)
    for c in range(N_LANE_CHUNKS):
        cext_ref[c, HIST_C:HIST_C + T, :] = cx[:, c * LANES:(c + 1) * LANES]
    proj(COL_C)
    proj(COL_C + 3 * W_C)
    conv_a_chunk(3)
    for c in range(N_LANE_CHUNKS):
        lanes = slice(c * LANES, (c + 1) * LANES)
        acc = cext_ref[c, off_c:off_c + T, :] * ccw_ref[0:1, lanes]
        for k in range(1, CONV_C):
            acc = acc + cext_ref[c, off_c + k:off_c + k + T, :] * ccw_ref[k:k + 1, lanes]
        y_c = p_ref[:, COL_C + c * LANES:COL_C + (c + 1) * LANES] * acc * _silu(
            p_ref[:, COL_C + 3 * W_C + c * LANES:COL_C + 3 * W_C + (c + 1) * LANES])
        mix_ref[:, W_A + W_B + c * LANES:W_A + W_B + (c + 1) * LANES] = y_c.astype(BF16)

    out_bc = jnp.dot(mix_ref[:, W_A:D_MIX], wout_ref[W_A:D_MIX, :], preferred_element_type=F32)
    y_a = _silu(_layernorm(aconv_ref[...], alng_ref[...], alnb_ref[...])) * _silu(pcol(COL_A + 2 * W_A))
    mix_ref[:, 0:W_A] = y_a.astype(BF16)
    post = gate * gpost_ref[...]
    for h0 in range(0, T, EDGE_ROWS):
        hr = slice(h0, h0 + EDGE_ROWS)
        mix = out_bc[hr] + jnp.dot(mix_ref[hr, 0:W_A], wout_ref[0:W_A, :], preferred_element_type=F32)
        r2 = lax.rsqrt(jnp.mean(mix * mix, axis=-1, keepdims=True) + RMS_EPS)
        y_ref[hr, :] = x_ref[hr, :] + (mix * r2) * post

    @pl.when(j == pl.num_programs(1) - 1)
    def _():
        for c in range(N_LANE_CHUNKS):
            ca_ref[:, c * LANES:(c + 1) * LANES] = aext_ref[c, T + off_a:T + HIST_A, :]
            cc_ref[:, c * LANES:(c + 1) * LANES] = cext_ref[c, T + off_c:T + HIST_C, :]

    for c in range(N_LANE_CHUNKS):
        aext_ref[c, 0:HIST_A, :] = aext_ref[c, T:T + HIST_A, :]
        cext_ref[c, 0:HIST_C, :] = cext_ref[c, T:T + HIST_C, :]


def _prompt_layer(l, x, mod4, g_pre, g_post, w_in_b, w_out_b, w_in, w_out, a_conv_w, a_conv_b,
                  a_ln_g, a_ln_b, b_ln_g, b_ln_b, b_ws, bsb, c_conv_w):
    nt = SEQ // TILE_T
    n_steps = BATCH * nt
    cast_next = l + 1 < DEPTH
    rows_in, rows_out = D_MODEL // n_steps, D_MIX // n_steps
    assert rows_in * n_steps == D_MODEL and rows_out * n_steps == D_MIX
    assert rows_in % (2 * SUBLANES) == 0 and rows_out % (2 * SUBLANES) == 0
    row = lambda w: pl.BlockSpec((None, 1, w), lambda b, j: (l, 0, 0))
    in_specs = [
        pl.BlockSpec((TILE_T, D_MODEL), lambda b, j: (b * nt + j, 0)),
        pl.BlockSpec((None, None, 3, D_MODEL), lambda b, j: (l, DEC_BATCH + b, 0, 0)),
        row(D_MODEL), row(D_MODEL),
        pl.BlockSpec(memory_space=pl.ANY),
        pl.BlockSpec(memory_space=pl.ANY),
        pl.BlockSpec((None, CONV_A, W_A), lambda b, j: (l, 0, 0)),
        row(W_A), row(W_A), row(W_A), row(W_B), row(W_B),
        pl.BlockSpec((None, HEADS_B, CHUNK, CHUNK), lambda b, j: (l, 0, 0, 0)),
        pl.BlockSpec((None, CHUNK, W_B), lambda b, j: (l, 0, 0)),
        pl.BlockSpec((None, CONV_C, W_C), lambda b, j: (l, 0, 0)),
    ]
    out_specs = [
        pl.BlockSpec((TILE_T, D_MODEL), lambda b, j: (b * nt + j, 0)),
        pl.BlockSpec((None, CONV_A - 1, W_A), lambda b, j: (b, 0, 0)),
        pl.BlockSpec((None, CONV_C - 1, W_C), lambda b, j: (b, 0, 0)),
    ]
    out_shape = [
        jax.ShapeDtypeStruct((BATCH * SEQ, D_MODEL), F32),
        jax.ShapeDtypeStruct((BATCH, CONV_A - 1, W_A), F32),
        jax.ShapeDtypeStruct((BATCH, CONV_C - 1, W_C), F32),
    ]
    args = [x, mod4, g_pre, g_post, w_in_b, w_out_b, a_conv_w, a_conv_b,
            a_ln_g, a_ln_b, b_ln_g, b_ln_b, b_ws, bsb, c_conv_w]
    if cast_next:
        in_specs += [
            pl.BlockSpec((None, rows_in, P_IN), lambda b, j: (l + 1, b * nt + j, 0)),
            pl.BlockSpec((None, rows_out, D_MODEL), lambda b, j: (l + 1, b * nt + j, 0)),
        ]
        out_specs += [
            pl.BlockSpec((rows_in, P_IN), lambda b, j: (b * nt + j, 0)),
            pl.BlockSpec((rows_out, D_MODEL), lambda b, j: (b * nt + j, 0)),
        ]
        out_shape += [
            jax.ShapeDtypeStruct((D_MODEL, P_IN), BF16),
            jax.ShapeDtypeStruct((D_MIX, D_MODEL), BF16),
        ]
        args += [w_in, w_out]
    return pl.pallas_call(
        functools.partial(_prompt_kernel, cast_next),
        grid=(BATCH, nt),
        in_specs=in_specs,
        out_specs=out_specs,
        out_shape=out_shape,
        scratch_shapes=[
            pltpu.VMEM((D_MODEL, P_IN), BF16),
            pltpu.VMEM((D_MIX, D_MODEL), BF16),
            pltpu.SemaphoreType.DMA((2,)),
            pltpu.VMEM((TILE_T, D_MODEL), BF16),
            pltpu.VMEM((TILE_T, P_IN), F32),
            pltpu.VMEM((N_LANE_CHUNKS, TILE_T + HIST_A, LANES), F32),
            pltpu.VMEM((N_LANE_CHUNKS, TILE_T + HIST_C, LANES), F32),
            pltpu.VMEM((TILE_T, W_A), F32),
            pltpu.VMEM((TILE_T, W_B), BF16),
            pltpu.VMEM((TILE_T, D_MIX), BF16),
        ],
        compiler_params=pltpu.CompilerParams(
            dimension_semantics=("arbitrary", "arbitrary"), vmem_limit_bytes=VMEM_LIMIT),
        name=f"prompt_layer{l}",
    )(*args)


def _sample_kernel(x_ref, mod_ref, gpre_ref, gpost_ref, *rest):
    w_hbm, rest = rest[:2 * DEPTH], rest[2 * DEPTH:]
    (acw_ref, acb_ref, alng_ref, alnb_ref, blng_ref, blnb_ref, wss_ref, bss_ref, ccw_ref, ha_ref, hc_ref,
     y_ref, ca_ref, cc_ref, v_ref, win_ring, wout_ring, w_sem, mix_ref) = rest
    S, N = DEC_SEQ, SAMPLE_N
    l = pl.program_id(1)
    step = pl.program_id(0) * DEPTH + l
    n_steps = pl.num_programs(0) * DEPTH
    slot = step % 2

    def w_copies(layer, to_slot):
        return (pltpu.make_async_copy(w_hbm[layer], win_ring.at[to_slot], w_sem.at[0, to_slot]),
                pltpu.make_async_copy(w_hbm[DEPTH + layer], wout_ring.at[to_slot], w_sem.at[1, to_slot]))

    @pl.when(step == 0)
    def _():
        for cp in w_copies(0, 0):
            cp.start()

    for layer in range(DEPTH):
        @pl.when((step + 1 < n_steps) & ((l + 1) % DEPTH == layer))
        def _():
            for cp in w_copies(layer, 1 - slot):
                cp.start()

    for cp in w_copies(0, slot):
        cp.wait()
    win_ref = win_ring.at[slot]
    wout_ref = wout_ring.at[slot]

    @pl.when(l == 0)
    def _():
        y_ref[...] = x_ref[...]

    shift = mod_ref[:, 0:D_MODEL]
    scale = mod_ref[:, D_MODEL:2 * D_MODEL]
    gate = mod_ref[:, 2 * D_MODEL:3 * D_MODEL]
    pre = gpre_ref[...] * (1.0 + scale)
    xs = [y_ref[t] for t in range(S)]
    hs = []
    for t in range(S):
        r = lax.rsqrt(jnp.mean(xs[t] * xs[t], axis=-1, keepdims=True) + RMS_EPS)
        hs.append(((xs[t] * r) * pre + shift).astype(BF16))
    hb = jnp.concatenate(hs, axis=0)

    pa = jnp.dot(hb, win_ref[:, COL_A:COL_A + 3 * W_A], preferred_element_type=F32)
    a = pa[:, 0:W_A] * jax.nn.sigmoid(pa[:, W_A:2 * W_A])
    a_z = pa[:, 2 * W_A:3 * W_A]
    n_hist = CONV_A - 1
    a_ext = lambda i: ha_ref[i] if i < n_hist else a[(i - n_hist) * N:(i - n_hist + 1) * N]
    for k in range(n_hist):
        ca_ref[k] = a_ext(k + S)
    for t in range(S):
        acc = jnp.broadcast_to(acb_ref[...], (N, W_A))
        for k in range(CONV_A):
            acc = acc + a_ext(t + k) * acw_ref[k:k + 1, :]
        ln = _layernorm(acc, alng_ref[...], alnb_ref[...])
        y_a = _silu(ln) * _silu(a_z[t * N:(t + 1) * N])
        mix_ref[t * N:(t + 1) * N, 0:W_A] = y_a.astype(BF16)

    pb = jnp.dot(hb, win_ref[:, COL_B:COL_B + 3 * W_B], preferred_element_type=F32)
    v_n = _layernorm(pb[:, W_B:2 * W_B], blng_ref[...], blnb_ref[...])
    vs = [v_n[t * N:(t + 1) * N] for t in range(S)]
    for t in range(S):
        v_ref[t] = vs[t]
        s = jnp.broadcast_to(bss_ref[t:t + 1, :], (N, W_B))
        for u in range(t + 1):
            s = s + vs[u] * wss_ref[t, u:u + 1, :]
        rows = slice(t * N, (t + 1) * N)
        y_b = pb[rows, 0:W_B] * s * _silu(pb[rows, 2 * W_B:3 * W_B])
        mix_ref[rows, W_A:W_A + W_B] = y_b.astype(BF16)

    pc = jnp.dot(hb, win_ref[:, COL_C:COL_C + 4 * W_C], preferred_element_type=F32)
    cx = pc[:, W_C:2 * W_C] * pc[:, 2 * W_C:3 * W_C]
    n_hc = CONV_C - 1
    c_ext = lambda i: hc_ref[i] if i < n_hc else cx[(i - n_hc) * N:(i - n_hc + 1) * N]
    for k in range(n_hc):
        cc_ref[k] = c_ext(k + S)
    for t in range(S):
        acc = c_ext(t) * ccw_ref[0:1, :]
        for k in range(1, CONV_C):
            acc = acc + c_ext(t + k) * ccw_ref[k:k + 1, :]
        rows = slice(t * N, (t + 1) * N)
        y_c = pc[rows, 0:W_C] * acc * _silu(pc[rows, 3 * W_C:4 * W_C])
        mix_ref[rows, W_A + W_B:D_MIX] = y_c.astype(BF16)

    mix = jnp.dot(mix_ref[...], wout_ref[...], preferred_element_type=F32)
    r2 = lax.rsqrt(jnp.mean(mix * mix, axis=-1, keepdims=True) + RMS_EPS)
    post = gate * gpost_ref[...]
    for t in range(S):
        rows = slice(t * N, (t + 1) * N)
        y_ref[t] = xs[t] + (mix[rows] * r2[rows]) * post


def _sample_layers(x_t, mod_all, g_pre, g_post, w_in_b, w_out_b, a_conv_w, a_conv_b,
                   a_ln_g, a_ln_b, b_ln_g, b_ln_b, wss, bss, c_conv_w, ha_t, hc_t):
    S, N = DEC_SEQ, SAMPLE_N
    nsplit = DEC_BATCH // N
    row = lambda w: pl.BlockSpec((None, 1, w), lambda n, l: (l, 0, 0))
    lay3 = lambda a, b: pl.BlockSpec((None, a, b), lambda n, l: (l, 0, 0))
    lay4 = lambda a, b, c: pl.BlockSpec((None, a, b, c), lambda n, l: (l, 0, 0, 0))
    seq4 = lambda a, c: pl.BlockSpec((None, a, N, c), lambda n, l: (l, 0, n, 0))
    return pl.pallas_call(
        _sample_kernel,
        grid=(nsplit, DEPTH),
        in_specs=[
            pl.BlockSpec((S, N, D_MODEL), lambda n, l: (0, n, 0)),
            pl.BlockSpec((None, N, 3 * D_MODEL), lambda n, l: (l, n, 0)),
            row(D_MODEL), row(D_MODEL),
            *[pl.BlockSpec(memory_space=pl.ANY)] * (2 * DEPTH),
            lay3(CONV_A, W_A),
            row(W_A), row(W_A), row(W_A), row(W_B), row(W_B),
            lay4(S, S, W_B), lay3(S, W_B), lay3(CONV_C, W_C),
            seq4(CONV_A - 1, W_A), seq4(CONV_C - 1, W_C),
        ],
        out_specs=[
            pl.BlockSpec((S, N, D_MODEL), lambda n, l: (0, n, 0)),
            seq4(CONV_A - 1, W_A), seq4(CONV_C - 1, W_C), seq4(S, W_B),
        ],
        out_shape=[
            jax.ShapeDtypeStruct((S, DEC_BATCH, D_MODEL), F32),
            jax.ShapeDtypeStruct((DEPTH, CONV_A - 1, DEC_BATCH, W_A), F32),
            jax.ShapeDtypeStruct((DEPTH, CONV_C - 1, DEC_BATCH, W_C), F32),
            jax.ShapeDtypeStruct((DEPTH, S, DEC_BATCH, W_B), F32),
        ],
        scratch_shapes=[
            pltpu.VMEM((2, D_MODEL, P_IN), BF16),
            pltpu.VMEM((2, D_MIX, D_MODEL), BF16),
            pltpu.SemaphoreType.DMA((2, 2)),
            pltpu.VMEM((S * N, D_MIX), BF16),
        ],
        compiler_params=pltpu.CompilerParams(
            dimension_semantics=("arbitrary", "arbitrary"), vmem_limit_bytes=VMEM_LIMIT),
        name="sample_layers",
    )(x_t, mod_all, g_pre, g_post, *w_in_b, *w_out_b, a_conv_w, a_conv_b,
      a_ln_g, a_ln_b, b_ln_g, b_ln_b, wss, bss, c_conv_w, ha_t, hc_t)


def kernel(x_prompt, x_sample, c_prompt, c_sample, state_conv_a, state_conv_c, w_mod, b_mod,
           g_pre, g_post, w_in, w_out, a_conv_w, a_conv_b, a_ln_g, a_ln_b, b_ln_g, b_ln_b,
           b_ws, b_bs, c_conv_w):
    S = DEC_SEQ
    w_in_b = [w_in[0].astype(BF16)]
    w_out_b = [w_out[0].astype(BF16)]
    r3 = lambda p: p.reshape(DEPTH, 1, p.shape[-1])
    g_pre3, g_post3 = r3(g_pre), r3(g_post)
    acb3, alng3, alnb3, blng3, blnb3 = r3(a_conv_b), r3(a_ln_g), r3(a_ln_b), r3(b_ln_g), r3(b_ln_b)

    c_all = jnp.concatenate([c_sample, c_prompt], axis=0)
    mod_all = _mod_call(c_all, w_mod, b_mod)
    mod4 = mod_all.reshape(DEPTH, N_ALL, 3, D_MODEL)

    bsb = jnp.repeat(jnp.swapaxes(b_bs, 1, 2), HEAD_W, axis=2)
    wss = jnp.repeat(jnp.transpose(b_ws[:, :, :S, :S], (0, 2, 3, 1)), HEAD_W, axis=3)
    bss = bsb[:, :S, :]

    xp = x_prompt.reshape(BATCH * SEQ, D_MODEL)
    ca_p, cc_p = [], []
    for l in range(DEPTH):
        outs = _prompt_layer(l, xp, mod4, g_pre3, g_post3, w_in_b[l], w_out_b[l], w_in, w_out, a_conv_w, acb3,
                             alng3, alnb3, blng3, blnb3, b_ws, bsb, c_conv_w)
        xp, ca, cc = outs[:3]
        ca_p.append(ca)
        cc_p.append(cc)
        if l + 1 < DEPTH:
            w_in_b.append(outs[3])
            w_out_b.append(outs[4])

    x_t = jnp.swapaxes(x_sample, 0, 1)
    ha_t = jnp.swapaxes(state_conv_a, 1, 2)
    hc_t = jnp.swapaxes(state_conv_c, 1, 2)
    y_t, ca_t, cc_t, v_t = _sample_layers(
        x_t, mod_all, g_pre3, g_post3, w_in_b, w_out_b, a_conv_w, acb3,
        alng3, alnb3, blng3, blnb3, wss, bss, c_conv_w, ha_t, hc_t)

    return (xp.reshape(BATCH, SEQ, D_MODEL),
            jnp.swapaxes(y_t, 0, 1),
            jnp.stack(ca_p), jnp.stack(cc_p),
            jnp.swapaxes(ca_t, 1, 2), jnp.swapaxes(cc_t, 1, 2), jnp.swapaxes(v_t, 1, 2))
```

```python
import functools

import jax
import jax.numpy as jnp
from jax import lax
from jax.experimental import pallas as pl
from jax.experimental.pallas import tpu as pltpu

D_MODEL = 1024
BATCH = 8
SEQ = 2048
DEPTH = 4
DEC_BATCH = 128
DEC_SEQ = 4
W_A = 512
W_B = 512
W_C = 512
D_MIX = W_A + W_B + W_C
P_IN = 3 * W_A + 3 * W_B + 4 * W_C
CONV_A = 31
CONV_C = 3
CHUNK = 128
HEADS_B = 4
HEAD_W = W_B // HEADS_B
RMS_EPS = 1e-6
LN_EPS = 1e-5

LANES = 128
SUBLANES = 8
N_LANE_CHUNKS = W_A // LANES

COL_A = 0
COL_B = 3 * W_A
COL_C = 3 * W_A + 3 * W_B

TILE_T = 512
HIST_A = 32
HIST_C = 8
CONV_ROWS = 16
EDGE_ROWS = TILE_T // 4
N_ALL = DEC_BATCH + BATCH
SAMPLE_N = 64

VMEM_LIMIT = 56 * 1024 * 1024

F32 = jnp.float32
BF16 = jnp.bfloat16


def _silu(x):
    return x * jax.nn.sigmoid(x)


def _layernorm(x, g, b):
    mu = jnp.mean(x, axis=-1, keepdims=True)
    xc = x - mu
    var = jnp.mean(xc * xc, axis=-1, keepdims=True)
    return xc * lax.rsqrt(var + LN_EPS) * g + b


def _mod_kernel(c_ref, w_ref, b_ref, o_ref):
    c = c_ref[...]
    s = _silu(c).astype(BF16)
    o_ref[...] = jnp.dot(s, w_ref[...].astype(BF16), preferred_element_type=F32) + b_ref[...]


def _mod_call(c_all, w_mod, b_mod):
    tn = D_MODEL
    return pl.pallas_call(
        _mod_kernel,
        grid=(DEPTH, 3 * D_MODEL // tn),
        in_specs=[
            pl.BlockSpec((N_ALL, D_MODEL), lambda l, n: (0, 0)),
            pl.BlockSpec((None, D_MODEL, tn), lambda l, n: (l, 0, n)),
            pl.BlockSpec((None, 1, tn), lambda l, n: (l, 0, n)),
        ],
        out_specs=pl.BlockSpec((None, N_ALL, tn), lambda l, n: (l, 0, n)),
        out_shape=jax.ShapeDtypeStruct((DEPTH, N_ALL, 3 * D_MODEL), F32),
        compiler_params=pltpu.CompilerParams(
            dimension_semantics=("arbitrary", "arbitrary"), vmem_limit_bytes=VMEM_LIMIT),
        name="mod",
    )(c_all, w_mod, b_mod.reshape(DEPTH, 1, 3 * D_MODEL))


def _prompt_kernel(cast_next, x_ref, mod_ref, gpre_ref, gpost_ref, win_hbm, wout_hbm,
                   acw_ref, acb_ref, alng_ref, alnb_ref, blng_ref, blnb_ref,
                   ws_ref, bsb_ref, ccw_ref, *rest):
    if cast_next:
        nwin_f32, nwout_f32, y_ref, ca_ref, cc_ref, nwin_b, nwout_b, *scratch = rest
        nwin_b[...] = nwin_f32[...].astype(BF16)
        nwout_b[...] = nwout_f32[...].astype(BF16)
    else:
        y_ref, ca_ref, cc_ref, *scratch = rest
    win_ref, wout_ref, w_sem, hb_ref, p_ref, aext_ref, cext_ref, aconv_ref, vb_ref, mix_ref = scratch
    T = TILE_T
    j = pl.program_id(1)
    off_a = HIST_A - (CONV_A - 1)
    off_c = HIST_C - (CONV_C - 1)

    @pl.when((pl.program_id(0) == 0) & (j == 0))
    def _():
        cp_in = pltpu.make_async_copy(win_hbm, win_ref, w_sem.at[0])
        cp_out = pltpu.make_async_copy(wout_hbm, wout_ref, w_sem.at[1])
        cp_in.start()
        cp_out.start()
        cp_in.wait()
        cp_out.wait()

    @pl.when(j == 0)
    def _():
        aext_ref[:, 0:HIST_A, :] = jnp.zeros((N_LANE_CHUNKS, HIST_A, LANES), F32)
        cext_ref[:, 0:HIST_C, :] = jnp.zeros((N_LANE_CHUNKS, HIST_C, LANES), F32)

    shift = mod_ref[0:1, :]
    scale = mod_ref[1:2, :]
    gate = mod_ref[2:3, :]
    pre = gpre_ref[...] * (1.0 + scale)
    for h0 in range(0, T, EDGE_ROWS):
        hr = slice(h0, h0 + EDGE_ROWS)
        x = x_ref[hr, :]
        r = lax.rsqrt(jnp.mean(x * x, axis=-1, keepdims=True) + RMS_EPS)
        hb_ref[hr, :] = ((x * r) * pre + shift).astype(BF16)

    def proj(col):
        p_ref[:, col:col + W_A] = jnp.dot(hb_ref[...], win_ref[:, col:col + W_A], preferred_element_type=F32)

    def pcol(col):
        return p_ref[:, col:col + W_A]

    def conv_a_chunk(c):
        lanes = slice(c * LANES, (c + 1) * LANES)
        for r0 in range(0, T, CONV_ROWS):
            acc = jnp.broadcast_to(acb_ref[:, lanes], (CONV_ROWS, LANES))
            for k in range(CONV_A):
                s0 = r0 + off_a + k
                acc = acc + aext_ref[c, s0:s0 + CONV_ROWS, :] * acw_ref[k:k + 1, lanes]
            aconv_ref[r0:r0 + CONV_ROWS, lanes] = acc

    for h0 in range(0, T, EDGE_ROWS):
        hr = slice(h0, h0 + EDGE_ROWS)
        for col in (COL_A, COL_A + W_A):
            p_ref[hr, col:col + W_A] = jnp.dot(hb_ref[hr, :], win_ref[:, col:col + W_A], preferred_element_type=F32)
    a = pcol(COL_A) * jax.nn.sigmoid(pcol(COL_A + W_A))
    for c in range(N_LANE_CHUNKS):
        aext_ref[c, HIST_A:HIST_A + T, :] = a[:, c * LANES:(c + 1) * LANES]
    proj(COL_A + 2 * W_A)
    conv_a_chunk(0)

    proj(COL_B + W_B)
    proj(COL_B)
    proj(COL_B + 2 * W_B)
    vb_ref[...] = _layernorm(pcol(COL_B + W_B), blng_ref[...], blnb_ref[...]).astype(BF16)
    conv_a_chunk(1)
    row_i = lax.broadcasted_iota(jnp.int32, (CHUNK, CHUNK), 0)
    col_i = lax.broadcasted_iota(jnp.int32, (CHUNK, CHUNK), 1)
    wm = [jnp.where(row_i >= col_i, ws_ref[hd], 0.0).astype(BF16) for hd in range(HEADS_B)]
    col_s = COL_B + W_B
    for ch in range(T // CHUNK):
        rows = slice(ch * CHUNK, (ch + 1) * CHUNK)
        for hd in range(HEADS_B):
            hcols = slice(hd * HEAD_W, (hd + 1) * HEAD_W)
            p_ref[rows, col_s + hd * HEAD_W:col_s + (hd + 1) * HEAD_W] = (
                jnp.dot(wm[hd], vb_ref[rows, hcols], preferred_element_type=F32) + bsb_ref[:, hcols])
    y_b = pcol(COL_B) * pcol(col_s) * _silu(pcol(COL_B + 2 * W_B))
    mix_ref[:, W_A:W_A + W_B] = y_b.astype(BF16)

    proj(COL_C + W_C)
    proj(COL_C + 2 * W_C)
    conv_a_chunk(2)
    cx = pcol(COL_C + W_C) * pcol(COL_C + 2 * W_C)
    for c in range(N_LANE_CHUNKS):
        cext_ref[c, HIST_C:HIST_C + T, :] = cx[:, c * LANES:(c + 1) * LANES]
    proj(COL_C)
    proj(COL_C + 3 * W_C)
    conv_a_chunk(3)
    for c in range(N_LANE_CHUNKS):
        lanes = slice(c * LANES, (c + 1) * LANES)
        acc = cext_ref[c, off_c:off_c + T, :] * ccw_ref[0:1, lanes]
        for k in range(1, CONV_C):
            acc = acc + cext_ref[c, off_c + k:off_c + k + T, :] * ccw_ref[k:k + 1, lanes]
        y_c = p_ref[:, COL_C + c * LANES:COL_C + (c + 1) * LANES] * acc * _silu(
            p_ref[:, COL_C + 3 * W_C + c * LANES:COL_C + 3 * W_C + (c + 1) * LANES])
        mix_ref[:, W_A + W_B + c * LANES:W_A + W_B + (c + 1) * LANES] = y_c.astype(BF16)

    out_bc = jnp.dot(mix_ref[:, W_A:D_MIX], wout_ref[W_A:D_MIX, :], preferred_element_type=F32)
    y_a = _silu(_layernorm(aconv_ref[...], alng_ref[...], alnb_ref[...])) * _silu(pcol(COL_A + 2 * W_A))
    mix_ref[:, 0:W_A] = y_a.astype(BF16)
    post = gate * gpost_ref[...]
    for h0 in range(0, T, EDGE_ROWS):
        hr = slice(h0, h0 + EDGE_ROWS)
        mix = out_bc[hr] + jnp.dot(mix_ref[hr, 0:W_A], wout_ref[0:W_A, :], preferred_element_type=F32)
        r2 = lax.rsqrt(jnp.mean(mix * mix, axis=-1, keepdims=True) + RMS_EPS)
        y_ref[hr, :] = x_ref[hr, :] + (mix * r2) * post

    @pl.when(j == pl.num_programs(1) - 1)
    def _():
        for c in range(N_LANE_CHUNKS):
            ca_ref[:, c * LANES:(c + 1) * LANES] = aext_ref[c, T + off_a:T + HIST_A, :]
            cc_ref[:, c * LANES:(c + 1) * LANES] = cext_ref[c, T + off_c:T + HIST_C, :]

    for c in range(N_LANE_CHUNKS):
        aext_ref[c, 0:HIST_A, :] = aext_ref[c, T:T + HIST_A, :]
        cext_ref[c, 0:HIST_C, :] = cext_ref[c, T:T + HIST_C, :]


def _prompt_layer(l, x, mod4, g_pre, g_post, w_in_b, w_out_b, w_in, w_out, a_conv_w, a_conv_b,
                  a_ln_g, a_ln_b, b_ln_g, b_ln_b, b_ws, bsb, c_conv_w):
    nt = SEQ // TILE_T
    n_steps = BATCH * nt
    cast_next = l + 1 < DEPTH
    rows_in, rows_out = D_MODEL // n_steps, D_MIX // n_steps
    assert rows_in * n_steps == D_MODEL and rows_out * n_steps == D_MIX
    assert rows_in % (2 * SUBLANES) == 0 and rows_out % (2 * SUBLANES) == 0
    row = lambda w: pl.BlockSpec((None, 1, w), lambda b, j: (l, 0, 0))
    in_specs = [
        pl.BlockSpec((TILE_T, D_MODEL), lambda b, j: (b * nt + j, 0)),
        pl.BlockSpec((None, None, 3, D_MODEL), lambda b, j: (l, DEC_BATCH + b, 0, 0)),
        row(D_MODEL), row(D_MODEL),
        pl.BlockSpec(memory_space=pl.ANY),
        pl.BlockSpec(memory_space=pl.ANY),
        pl.BlockSpec((None, CONV_A, W_A), lambda b, j: (l, 0, 0)),
        row(W_A), row(W_A), row(W_A), row(W_B), row(W_B),
        pl.BlockSpec((None, HEADS_B, CHUNK, CHUNK), lambda b, j: (l, 0, 0, 0)),
        pl.BlockSpec((None, CHUNK, W_B), lambda b, j: (l, 0, 0)),
        pl.BlockSpec((None, CONV_C, W_C), lambda b, j: (l, 0, 0)),
    ]
    out_specs = [
        pl.BlockSpec((TILE_T, D_MODEL), lambda b, j: (b * nt + j, 0)),
        pl.BlockSpec((None, CONV_A - 1, W_A), lambda b, j: (b, 0, 0)),
        pl.BlockSpec((None, CONV_C - 1, W_C), lambda b, j: (b, 0, 0)),
    ]
    out_shape = [
        jax.ShapeDtypeStruct((BATCH * SEQ, D_MODEL), F32),
        jax.ShapeDtypeStruct((BATCH, CONV_A - 1, W_A), F32),
        jax.ShapeDtypeStruct((BATCH, CONV_C - 1, W_C), F32),
    ]
    args = [x, mod4, g_pre, g_post, w_in_b, w_out_b, a_conv_w, a_conv_b,
            a_ln_g, a_ln_b, b_ln_g, b_ln_b, b_ws, bsb, c_conv_w]
    if cast_next:
        in_specs += [
            pl.BlockSpec((None, rows_in, P_IN), lambda b, j: (l + 1, b * nt + j, 0)),
            pl.BlockSpec((None, rows_out, D_MODEL), lambda b, j: (l + 1, b * nt + j, 0)),
        ]
        out_specs += [
            pl.BlockSpec((rows_in, P_IN), lambda b, j: (b * nt + j, 0)),
            pl.BlockSpec((rows_out, D_MODEL), lambda b, j: (b * nt + j, 0)),
        ]
        out_shape += [
            jax.ShapeDtypeStruct((D_MODEL, P_IN), BF16),
            jax.ShapeDtypeStruct((D_MIX, D_MODEL), BF16),
        ]
        args += [w_in, w_out]
    return pl.pallas_call(
        functools.partial(_prompt_kernel, cast_next),
        grid=(BATCH, nt),
        in_specs=in_specs,
        out_specs=out_specs,
        out_shape=out_shape,
        scratch_shapes=[
            pltpu.VMEM((D_MODEL, P_IN), BF16),
            pltpu.VMEM((D_MIX, D_MODEL), BF16),
            pltpu.SemaphoreType.DMA((2,)),
            pltpu.VMEM((TILE_T, D_MODEL), BF16),
            pltpu.VMEM((TILE_T, P_IN), F32),
            pltpu.VMEM((N_LANE_CHUNKS, TILE_T + HIST_A, LANES), F32),
            pltpu.VMEM((N_LANE_CHUNKS, TILE_T + HIST_C, LANES), F32),
            pltpu.VMEM((TILE_T, W_A), F32),
            pltpu.VMEM((TILE_T, W_B), BF16),
            pltpu.VMEM((TILE_T, D_MIX), BF16),
        ],
        compiler_params=pltpu.CompilerParams(
            dimension_semantics=("arbitrary", "arbitrary"), vmem_limit_bytes=VMEM_LIMIT),
        name=f"prompt_layer{l}",
    )(*args)


def _sample_kernel(x_ref, mod_ref, gpre_ref, gpost_ref, *rest):
    w_hbm, rest = rest[:2 * DEPTH], rest[2 * DEPTH:]
    (acw_ref, acb_ref, alng_ref, alnb_ref, blng_ref, blnb_ref, wss_ref, bss_ref, ccw_ref, ha_ref, hc_ref,
     y_ref, ca_ref, cc_ref, v_ref, win_ring, wout_ring, w_sem, xs_ref, mix_ref) = rest
    S, N = DEC_SEQ, SAMPLE_N
    l = pl.program_id(0)
    n = pl.program_id(1)
    slot = l % 2

    def w_copies(layer, to_slot):
        return (pltpu.make_async_copy(w_hbm[layer], win_ring.at[to_slot], w_sem.at[0, to_slot]),
                pltpu.make_async_copy(w_hbm[DEPTH + layer], wout_ring.at[to_slot], w_sem.at[1, to_slot]))

    @pl.when(n == 0)
    def _():
        @pl.when(l == 0)
        def _():
            for cp in w_copies(0, 0):
                cp.start()

        for layer in range(1, DEPTH):
            @pl.when(l + 1 == layer)
            def _():
                for cp in w_copies(layer, layer % 2):
                    cp.start()

        for cp in w_copies(0, slot):
            cp.wait()

    win_ref = win_ring.at[slot]
    wout_ref = wout_ring.at[slot]

    @pl.when(l == 0)
    def _():
        xs_ref[n] = x_ref[...]

    shift = mod_ref[:, 0:D_MODEL]
    scale = mod_ref[:, D_MODEL:2 * D_MODEL]
    gate = mod_ref[:, 2 * D_MODEL:3 * D_MODEL]
    pre = gpre_ref[...] * (1.0 + scale)
    xs = [xs_ref[n, t] for t in range(S)]
    hs = []
    for t in range(S):
        r = lax.rsqrt(jnp.mean(xs[t] * xs[t], axis=-1, keepdims=True) + RMS_EPS)
        hs.append(((xs[t] * r) * pre + shift).astype(BF16))
    hb = jnp.concatenate(hs, axis=0)

    pa = jnp.dot(hb, win_ref[:, COL_A:COL_A + 3 * W_A], preferred_element_type=F32)
    a = pa[:, 0:W_A] * jax.nn.sigmoid(pa[:, W_A:2 * W_A])
    a_z = pa[:, 2 * W_A:3 * W_A]
    n_hist = CONV_A - 1
    a_ext = lambda i: ha_ref[i] if i < n_hist else a[(i - n_hist) * N:(i - n_hist + 1) * N]
    for k in range(n_hist):
        ca_ref[k] = a_ext(k + S)
    for t in range(S):
        acc = jnp.broadcast_to(acb_ref[...], (N, W_A))
        for k in range(CONV_A):
            acc = acc + a_ext(t + k) * acw_ref[k:k + 1, :]
        ln = _layernorm(acc, alng_ref[...], alnb_ref[...])
        y_a = _silu(ln) * _silu(a_z[t * N:(t + 1) * N])
        mix_ref[t * N:(t + 1) * N, 0:W_A] = y_a.astype(BF16)

    pb = jnp.dot(hb, win_ref[:, COL_B:COL_B + 3 * W_B], preferred_element_type=F32)
    v_n = _layernorm(pb[:, W_B:2 * W_B], blng_ref[...], blnb_ref[...])
    vs = [v_n[t * N:(t + 1) * N] for t in range(S)]
    for t in range(S):
        v_ref[t] = vs[t]
        s = jnp.broadcast_to(bss_ref[t:t + 1, :], (N, W_B))
        for u in range(t + 1):
            s = s + vs[u] * wss_ref[t, u:u + 1, :]
        rows = slice(t * N, (t + 1) * N)
        y_b = pb[rows, 0:W_B] * s * _silu(pb[rows, 2 * W_B:3 * W_B])
        mix_ref[rows, W_A:W_A + W_B] = y_b.astype(BF16)

    pc = jnp.dot(hb, win_ref[:, COL_C:COL_C + 4 * W_C], preferred_element_type=F32)
    cx = pc[:, W_C:2 * W_C] * pc[:, 2 * W_C:3 * W_C]
    n_hc = CONV_C - 1
    c_ext = lambda i: hc_ref[i] if i < n_hc else cx[(i - n_hc) * N:(i - n_hc + 1) * N]
    for k in range(n_hc):
        cc_ref[k] = c_ext(k + S)
    for t in range(S):
        acc = c_ext(t) * ccw_ref[0:1, :]
        for k in range(1, CONV_C):
            acc = acc + c_ext(t + k) * ccw_ref[k:k + 1, :]
        rows = slice(t * N, (t + 1) * N)
        y_c = pc[rows, 0:W_C] * acc * _silu(pc[rows, 3 * W_C:4 * W_C])
        mix_ref[rows, W_A + W_B:D_MIX] = y_c.astype(BF16)

    mix = jnp.dot(mix_ref[...], wout_ref[...], preferred_element_type=F32)
    r2 = lax.rsqrt(jnp.mean(mix * mix, axis=-1, keepdims=True) + RMS_EPS)
    post = gate * gpost_ref[...]
    ys = [xs[t] + (mix[t * N:(t + 1) * N] * r2[t * N:(t + 1) * N]) * post for t in range(S)]
    for t in range(S):
        xs_ref[n, t] = ys[t]

    @pl.when(l == DEPTH - 1)
    def _():
        for t in range(S):
            y_ref[t] = ys[t]


def _sample_layers(x_t, mod_all, g_pre, g_post, w_in_b, w_out_b, a_conv_w, a_conv_b,
                   a_ln_g, a_ln_b, b_ln_g, b_ln_b, wss, bss, c_conv_w, ha_t, hc_t):
    S, N = DEC_SEQ, SAMPLE_N
    nsplit = DEC_BATCH // N
    row = lambda w: pl.BlockSpec((None, 1, w), lambda l, n: (l, 0, 0))
    lay3 = lambda a, b: pl.BlockSpec((None, a, b), lambda l, n: (l, 0, 0))
    lay4 = lambda a, b, c: pl.BlockSpec((None, a, b, c), lambda l, n: (l, 0, 0, 0))
    seq4 = lambda a, c: pl.BlockSpec((None, a, N, c), lambda l, n: (l, 0, n, 0))
    return pl.pallas_call(
        _sample_kernel,
        grid=(DEPTH, nsplit),
        in_specs=[
            pl.BlockSpec((S, N, D_MODEL), lambda l, n: (0, jnp.where(l == 0, n, 0), 0)),
            pl.BlockSpec((None, N, 3 * D_MODEL), lambda l, n: (l, n, 0)),
            row(D_MODEL), row(D_MODEL),
            *[pl.BlockSpec(memory_space=pl.ANY)] * (2 * DEPTH),
            lay3(CONV_A, W_A),
            row(W_A), row(W_A), row(W_A), row(W_B), row(W_B),
            lay4(S, S, W_B), lay3(S, W_B), lay3(CONV_C, W_C),
            seq4(CONV_A - 1, W_A), seq4(CONV_C - 1, W_C),
        ],
        out_specs=[
            pl.BlockSpec((S, N, D_MODEL), lambda l, n: (0, jnp.where(l == DEPTH - 1, n, 0), 0)),
            seq4(CONV_A - 1, W_A), seq4(CONV_C - 1, W_C), seq4(S, W_B),
        ],
        out_shape=[
            jax.ShapeDtypeStruct((S, DEC_BATCH, D_MODEL), F32),
            jax.ShapeDtypeStruct((DEPTH, CONV_A - 1, DEC_BATCH, W_A), F32),
            jax.ShapeDtypeStruct((DEPTH, CONV_C - 1, DEC_BATCH, W_C), F32),
            jax.ShapeDtypeStruct((DEPTH, S, DEC_BATCH, W_B), F32),
        ],
        scratch_shapes=[
            pltpu.VMEM((2, D_MODEL, P_IN), BF16),
            pltpu.VMEM((2, D_MIX, D_MODEL), BF16),
            pltpu.SemaphoreType.DMA((2, 2)),
            pltpu.VMEM((nsplit, S, N, D_MODEL), F32),
            pltpu.VMEM((S * N, D_MIX), BF16),
        ],
        compiler_params=pltpu.CompilerParams(
            dimension_semantics=("arbitrary", "arbitrary"), vmem_limit_bytes=VMEM_LIMIT),
        name="sample_layers",
    )(x_t, mod_all, g_pre, g_post, *w_in_b, *w_out_b, a_conv_w, a_conv_b,
      a_ln_g, a_ln_b, b_ln_g, b_ln_b, wss, bss, c_conv_w, ha_t, hc_t)


def kernel(x_prompt, x_sample, c_prompt, c_sample, state_conv_a, state_conv_c, w_mod, b_mod,
           g_pre, g_post, w_in, w_out, a_conv_w, a_conv_b, a_ln_g, a_ln_b, b_ln_g, b_ln_b,
           b_ws, b_bs, c_conv_w):
    S = DEC_SEQ
    w_in_b = [w_in[0].astype(BF16)]
    w_out_b = [w_out[0].astype(BF16)]
    r3 = lambda p: p.reshape(DEPTH, 1, p.shape[-1])
    g_pre3, g_post3 = r3(g_pre), r3(g_post)
    acb3, alng3, alnb3, blng3, blnb3 = r3(a_conv_b), r3(a_ln_g), r3(a_ln_b), r3(b_ln_g), r3(b_ln_b)

    c_all = jnp.concatenate([c_sample, c_prompt], axis=0)
    mod_all = _mod_call(c_all, w_mod, b_mod)
    mod4 = mod_all.reshape(DEPTH, N_ALL, 3, D_MODEL)

    bsb = jnp.repeat(jnp.swapaxes(b_bs, 1, 2), HEAD_W, axis=2)
    wss = jnp.repeat(jnp.transpose(b_ws[:, :, :S, :S], (0, 2, 3, 1)), HEAD_W, axis=3)
    bss = bsb[:, :S, :]

    xp = x_prompt.reshape(BATCH * SEQ, D_MODEL)
    ca_p, cc_p = [], []
    for l in range(DEPTH):
        outs = _prompt_layer(l, xp, mod4, g_pre3, g_post3, w_in_b[l], w_out_b[l], w_in, w_out, a_conv_w, acb3,
                             alng3, alnb3, blng3, blnb3, b_ws, bsb, c_conv_w)
        xp, ca, cc = outs[:3]
        ca_p.append(ca)
        cc_p.append(cc)
        if l + 1 < DEPTH:
            w_in_b.append(outs[3])
            w_out_b.append(outs[4])

    x_t = jnp.swapaxes(x_sample, 0, 1)
    ha_t = jnp.swapaxes(state_conv_a, 1, 2)
    hc_t = jnp.swapaxes(state_conv_c, 1, 2)
    y_t, ca_t, cc_t, v_t = _sample_layers(
        x_t, mod_all, g_pre3, g_post3, w_in_b, w_out_b, a_conv_w, acb3,
        alng3, alnb3, blng3, blnb3, wss, bss, c_conv_w, ha_t, hc_t)

    return (xp.reshape(BATCH, SEQ, D_MODEL),
            jnp.swapaxes(y_t, 0, 1),
            jnp.stack(ca_p), jnp.stack(cc_p),
            jnp.swapaxes(ca_t, 1, 2), jnp.swapaxes(cc_t, 1, 2), jnp.swapaxes(v_t, 1, 2))
```

```python
import functools

import jax
import jax.numpy as jnp
from jax import lax
from jax.experimental import pallas as pl
from jax.experimental.pallas import tpu as pltpu

D_MODEL = 1024
BATCH = 8
SEQ = 2048
DEPTH = 4
DEC_BATCH = 128
DEC_SEQ = 4
W_A = 512
W_B = 512
W_C = 512
D_MIX = W_A + W_B + W_C
P_IN = 3 * W_A + 3 * W_B + 4 * W_C
CONV_A = 31
CONV_C = 3
CHUNK = 128
HEADS_B = 4
HEAD_W = W_B // HEADS_B
RMS_EPS = 1e-6
LN_EPS = 1e-5

LANES = 128
SUBLANES = 8
N_LANE_CHUNKS = W_A // LANES

COL_A = 0
COL_B = 3 * W_A
COL_C = 3 * W_A + 3 * W_B

TILE_T = 512
HIST_A = 32
HIST_C = 8
CONV_ROWS = 16
EDGE_ROWS = TILE_T // 4
N_ALL = DEC_BATCH + BATCH
MOD_COL_BLOCKS = 4
SAMPLE_N = 64

VMEM_LIMIT = 56 * 1024 * 1024

F32 = jnp.float32
BF16 = jnp.bfloat16


def _silu(x):
    return x * jax.nn.sigmoid(x)


def _layernorm(x, g, b):
    mu = jnp.mean(x, axis=-1, keepdims=True)
    xc = x - mu
    var = jnp.mean(xc * xc, axis=-1, keepdims=True)
    return xc * lax.rsqrt(var + LN_EPS) * g + b


def _mod_kernel(c_ref, w_ref, b_ref, win0_f32, wout0_f32, o_ref, win0_b, wout0_b):
    c = c_ref[...]
    s = _silu(c).astype(BF16)
    o_ref[...] = jnp.dot(s, w_ref[...].astype(BF16), preferred_element_type=F32) + b_ref[...]
    win0_b[...] = win0_f32[...].astype(BF16)
    wout0_b[...] = wout0_f32[...].astype(BF16)


def _mod_call(c_all, w_mod, b_mod, w_in, w_out):
    n_col = MOD_COL_BLOCKS
    tn = 3 * D_MODEL // n_col
    n_steps = DEPTH * n_col
    rows_in, rows_out = D_MODEL // n_steps, D_MIX // n_steps
    assert tn * n_col == 3 * D_MODEL and tn % LANES == 0
    assert rows_in * n_steps == D_MODEL and rows_out * n_steps == D_MIX
    assert rows_in % (2 * SUBLANES) == 0 and rows_out % (2 * SUBLANES) == 0
    return pl.pallas_call(
        _mod_kernel,
        grid=(DEPTH, n_col),
        in_specs=[
            pl.BlockSpec((N_ALL, D_MODEL), lambda l, n: (0, 0)),
            pl.BlockSpec((None, D_MODEL, tn), lambda l, n: (l, 0, n)),
            pl.BlockSpec((None, 1, tn), lambda l, n: (l, 0, n)),
            pl.BlockSpec((None, rows_in, P_IN), lambda l, n: (0, l * n_col + n, 0)),
            pl.BlockSpec((None, rows_out, D_MODEL), lambda l, n: (0, l * n_col + n, 0)),
        ],
        out_specs=[
            pl.BlockSpec((None, N_ALL, tn), lambda l, n: (l, 0, n)),
            pl.BlockSpec((rows_in, P_IN), lambda l, n: (l * n_col + n, 0)),
            pl.BlockSpec((rows_out, D_MODEL), lambda l, n: (l * n_col + n, 0)),
        ],
        out_shape=[
            jax.ShapeDtypeStruct((DEPTH, N_ALL, 3 * D_MODEL), F32),
            jax.ShapeDtypeStruct((D_MODEL, P_IN), BF16),
            jax.ShapeDtypeStruct((D_MIX, D_MODEL), BF16),
        ],
        compiler_params=pltpu.CompilerParams(
            dimension_semantics=("arbitrary", "arbitrary"), vmem_limit_bytes=VMEM_LIMIT),
        name="mod",
    )(c_all, w_mod, b_mod.reshape(DEPTH, 1, 3 * D_MODEL), w_in, w_out)


def _prompt_kernel(cast_next, x_ref, mod_ref, gpre_ref, gpost_ref, win_hbm, wout_hbm,
                   acw_ref, acb_ref, alng_ref, alnb_ref, blng_ref, blnb_ref,
                   ws_ref, bsb_ref, ccw_ref, *rest):
    if cast_next:
        nwin_f32, nwout_f32, y_ref, ca_ref, cc_ref, nwin_b, nwout_b, *scratch = rest
        nwin_b[...] = nwin_f32[...].astype(BF16)
        nwout_b[...] = nwout_f32[...].astype(BF16)
    else:
        y_ref, ca_ref, cc_ref, *scratch = rest
    win_ref, wout_ref, w_sem, hb_ref, p_ref, aext_ref, cext_ref, aconv_ref, vb_ref, mix_ref = scratch
    T = TILE_T
    j = pl.program_id(1)
    off_a = HIST_A - (CONV_A - 1)
    off_c = HIST_C - (CONV_C - 1)

    @pl.when((pl.program_id(0) == 0) & (j == 0))
    def _():
        cp_in = pltpu.make_async_copy(win_hbm, win_ref, w_sem.at[0])
        cp_out = pltpu.make_async_copy(wout_hbm, wout_ref, w_sem.at[1])
        cp_in.start()
        cp_out.start()
        cp_in.wait()
        cp_out.wait()

    @pl.when(j == 0)
    def _():
        aext_ref[:, 0:HIST_A, :] = jnp.zeros((N_LANE_CHUNKS, HIST_A, LANES), F32)
        cext_ref[:, 0:HIST_C, :] = jnp.zeros((N_LANE_CHUNKS, HIST_C, LANES), F32)

    shift = mod_ref[0:1, :]
    scale = mod_ref[1:2, :]
    gate = mod_ref[2:3, :]
    pre = gpre_ref[...] * (1.0 + scale)
    for h0 in range(0, T, EDGE_ROWS):
        hr = slice(h0, h0 + EDGE_ROWS)
        x = x_ref[hr, :]
        r = lax.rsqrt(jnp.mean(x * x, axis=-1, keepdims=True) + RMS_EPS)
        hb_ref[hr, :] = ((x * r) * pre + shift).astype(BF16)

    def proj(col):
        p_ref[:, col:col + W_A] = jnp.dot(hb_ref[...], win_ref[:, col:col + W_A], preferred_element_type=F32)

    def pcol(col):
        return p_ref[:, col:col + W_A]

    def conv_a_chunk(c):
        lanes = slice(c * LANES, (c + 1) * LANES)
        for r0 in range(0, T, CONV_ROWS):
            acc = jnp.broadcast_to(acb_ref[:, lanes], (CONV_ROWS, LANES))
            for k in range(CONV_A):
                s0 = r0 + off_a + k
                acc = acc + aext_ref[c, s0:s0 + CONV_ROWS, :] * acw_ref[k:k + 1, lanes]
            aconv_ref[r0:r0 + CONV_ROWS, lanes] = acc

    for h0 in range(0, T, EDGE_ROWS):
        hr = slice(h0, h0 + EDGE_ROWS)
        for col in (COL_A, COL_A + W_A):
            p_ref[hr, col:col + W_A] = jnp.dot(hb_ref[hr, :], win_ref[:, col:col + W_A], preferred_element_type=F32)
    a = pcol(COL_A) * jax.nn.sigmoid(pcol(COL_A + W_A))
    for c in range(N_LANE_CHUNKS):
        aext_ref[c, HIST_A:HIST_A + T, :] = a[:, c * LANES:(c + 1) * LANES]
    proj(COL_A + 2 * W_A)
    conv_a_chunk(0)

    proj(COL_B + W_B)
    proj(COL_B)
    proj(COL_B + 2 * W_B)
    vb_ref[...] = _layernorm(pcol(COL_B + W_B), blng_ref[...], blnb_ref[...]).astype(BF16)
    conv_a_chunk(1)
    row_i = lax.broadcasted_iota(jnp.int32, (CHUNK, CHUNK), 0)
    col_i = lax.broadcasted_iota(jnp.int32, (CHUNK, CHUNK), 1)
    wm = [jnp.where(row_i >= col_i, ws_ref[hd], 0.0).astype(BF16) for hd in range(HEADS_B)]
    col_s = COL_B + W_B
    for ch in range(T // CHUNK):
        rows = slice(ch * CHUNK, (ch + 1) * CHUNK)
        for hd in range(HEADS_B):
            hcols = slice(hd * HEAD_W, (hd + 1) * HEAD_W)
            p_ref[rows, col_s + hd * HEAD_W:col_s + (hd + 1) * HEAD_W] = (
                jnp.dot(wm[hd], vb_ref[rows, hcols], preferred_element_type=F32) + bsb_ref[:, hcols])
    y_b = pcol(COL_B) * pcol(col_s) * _silu(pcol(COL_B + 2 * W_B))
    mix_ref[:, W_A:W_A + W_B] = y_b.astype(BF16)

    proj(COL_C + W_C)
    proj(COL_C + 2 * W_C)
    conv_a_chunk(2)
    cx = pcol(COL_C + W_C) * pcol(COL_C + 2 * W_C)
    for c in range(N_LANE_CHUNKS):
        cext_ref[c, HIST_C:HIST_C + T, :] = cx[:, c * LANES:(c + 1) * LANES]
    proj(COL_C)
    proj(COL_C + 3 * W_C)
    conv_a_chunk(3)
    for c in range(N_LANE_CHUNKS):
        lanes = slice(c * LANES, (c + 1) * LANES)
        acc = cext_ref[c, off_c:off_c + T, :] * ccw_ref[0:1, lanes]
        for k in range(1, CONV_C):
            acc = acc + cext_ref[c, off_c + k:off_c + k + T, :] * ccw_ref[k:k + 1, lanes]
        y_c = p_ref[:, COL_C + c * LANES:COL_C + (c + 1) * LANES] * acc * _silu(
            p_ref[:, COL_C + 3 * W_C + c * LANES:COL_C + 3 * W_C + (c + 1) * LANES])
        mix_ref[:, W_A + W_B + c * LANES:W_A + W_B + (c + 1) * LANES] = y_c.astype(BF16)

    out_bc = jnp.dot(mix_ref[:, W_A:D_MIX], wout_ref[W_A:D_MIX, :], preferred_element_type=F32)
    y_a = _silu(_layernorm(aconv_ref[...], alng_ref[...], alnb_ref[...])) * _silu(pcol(COL_A + 2 * W_A))
    mix_ref[:, 0:W_A] = y_a.astype(BF16)
    post = gate * gpost_ref[...]
    for h0 in range(0, T, EDGE_ROWS):
        hr = slice(h0, h0 + EDGE_ROWS)
        mix = out_bc[hr] + jnp.dot(mix_ref[hr, 0:W_A], wout_ref[0:W_A, :], preferred_element_type=F32)
        r2 = lax.rsqrt(jnp.mean(mix * mix, axis=-1, keepdims=True) + RMS_EPS)
        y_ref[hr, :] = x_ref[hr, :] + (mix * r2) * post

    @pl.when(j == pl.num_programs(1) - 1)
    def _():
        for c in range(N_LANE_CHUNKS):
            ca_ref[:, c * LANES:(c + 1) * LANES] = aext_ref[c, T + off_a:T + HIST_A, :]
            cc_ref[:, c * LANES:(c + 1) * LANES] = cext_ref[c, T + off_c:T + HIST_C, :]

    for c in range(N_LANE_CHUNKS):
        aext_ref[c, 0:HIST_A, :] = aext_ref[c, T:T + HIST_A, :]
        cext_ref[c, 0:HIST_C, :] = cext_ref[c, T:T + HIST_C, :]


def _prompt_layer(l, x, mod4, g_pre, g_post, w_in_b, w_out_b, w_in, w_out, a_conv_w, a_conv_b,
                  a_ln_g, a_ln_b, b_ln_g, b_ln_b, b_ws, bsb, c_conv_w):
    nt = SEQ // TILE_T
    n_steps = BATCH * nt
    cast_next = l + 1 < DEPTH
    rows_in, rows_out = D_MODEL // n_steps, D_MIX // n_steps
    assert rows_in * n_steps == D_MODEL and rows_out * n_steps == D_MIX
    assert rows_in % (2 * SUBLANES) == 0 and rows_out % (2 * SUBLANES) == 0
    row = lambda w: pl.BlockSpec((None, 1, w), lambda b, j: (l, 0, 0))
    in_specs = [
        pl.BlockSpec((TILE_T, D_MODEL), lambda b, j: (b * nt + j, 0)),
        pl.BlockSpec((None, None, 3, D_MODEL), lambda b, j: (l, DEC_BATCH + b, 0, 0)),
        row(D_MODEL), row(D_MODEL),
        pl.BlockSpec(memory_space=pl.ANY),
        pl.BlockSpec(memory_space=pl.ANY),
        pl.BlockSpec((None, CONV_A, W_A), lambda b, j: (l, 0, 0)),
        row(W_A), row(W_A), row(W_A), row(W_B), row(W_B),
        pl.BlockSpec((None, HEADS_B, CHUNK, CHUNK), lambda b, j: (l, 0, 0, 0)),
        pl.BlockSpec((None, CHUNK, W_B), lambda b, j: (l, 0, 0)),
        pl.BlockSpec((None, CONV_C, W_C), lambda b, j: (l, 0, 0)),
    ]
    out_specs = [
        pl.BlockSpec((TILE_T, D_MODEL), lambda b, j: (b * nt + j, 0)),
        pl.BlockSpec((None, CONV_A - 1, W_A), lambda b, j: (b, 0, 0)),
        pl.BlockSpec((None, CONV_C - 1, W_C), lambda b, j: (b, 0, 0)),
    ]
    out_shape = [
        jax.ShapeDtypeStruct((BATCH * SEQ, D_MODEL), F32),
        jax.ShapeDtypeStruct((BATCH, CONV_A - 1, W_A), F32),
        jax.ShapeDtypeStruct((BATCH, CONV_C - 1, W_C), F32),
    ]
    args = [x, mod4, g_pre, g_post, w_in_b, w_out_b, a_conv_w, a_conv_b,
            a_ln_g, a_ln_b, b_ln_g, b_ln_b, b_ws, bsb, c_conv_w]
    if cast_next:
        in_specs += [
            pl.BlockSpec((None, rows_in, P_IN), lambda b, j: (l + 1, b * nt + j, 0)),
            pl.BlockSpec((None, rows_out, D_MODEL), lambda b, j: (l + 1, b * nt + j, 0)),
        ]
        out_specs += [
            pl.BlockSpec((rows_in, P_IN), lambda b, j: (b * nt + j, 0)),
            pl.BlockSpec((rows_out, D_MODEL), lambda b, j: (b * nt + j, 0)),
        ]
        out_shape += [
            jax.ShapeDtypeStruct((D_MODEL, P_IN), BF16),
            jax.ShapeDtypeStruct((D_MIX, D_MODEL), BF16),
        ]
        args += [w_in, w_out]
    return pl.pallas_call(
        functools.partial(_prompt_kernel, cast_next),
        grid=(BATCH, nt),
        in_specs=in_specs,
        out_specs=out_specs,
        out_shape=out_shape,
        scratch_shapes=[
            pltpu.VMEM((D_MODEL, P_IN), BF16),
            pltpu.VMEM((D_MIX, D_MODEL), BF16),
            pltpu.SemaphoreType.DMA((2,)),
            pltpu.VMEM((TILE_T, D_MODEL), BF16),
            pltpu.VMEM((TILE_T, P_IN), F32),
            pltpu.VMEM((N_LANE_CHUNKS, TILE_T + HIST_A, LANES), F32),
            pltpu.VMEM((N_LANE_CHUNKS, TILE_T + HIST_C, LANES), F32),
            pltpu.VMEM((TILE_T, W_A), F32),
            pltpu.VMEM((TILE_T, W_B), BF16),
            pltpu.VMEM((TILE_T, D_MIX), BF16),
        ],
        compiler_params=pltpu.CompilerParams(
            dimension_semantics=("arbitrary", "arbitrary"), vmem_limit_bytes=VMEM_LIMIT),
        name=f"prompt_layer{l}",
    )(*args)


def _sample_kernel(x_ref, mod_ref, gpre_ref, gpost_ref, *rest):
    w_hbm, rest = rest[:2 * DEPTH], rest[2 * DEPTH:]
    (acw_ref, acb_ref, alng_ref, alnb_ref, blng_ref, blnb_ref, wss_ref, bss_ref, ccw_ref, ha_ref, hc_ref,
     y_ref, ca_ref, cc_ref, v_ref, win_ring, wout_ring, w_sem, xs_ref, mix_ref) = rest
    S, N = DEC_SEQ, SAMPLE_N
    l = pl.program_id(0)
    n = pl.program_id(1)
    slot = l % 2

    def w_copies(layer, to_slot):
        return (pltpu.make_async_copy(w_hbm[layer], win_ring.at[to_slot], w_sem.at[0, to_slot]),
                pltpu.make_async_copy(w_hbm[DEPTH + layer], wout_ring.at[to_slot], w_sem.at[1, to_slot]))

    @pl.when(n == 0)
    def _():
        @pl.when(l == 0)
        def _():
            for cp in w_copies(0, 0):
                cp.start()

        for layer in range(1, DEPTH):
            @pl.when(l + 1 == layer)
            def _():
                for cp in w_copies(layer, layer % 2):
                    cp.start()

        for cp in w_copies(0, slot):
            cp.wait()

    win_ref = win_ring.at[slot]
    wout_ref = wout_ring.at[slot]

    @pl.when(l == 0)
    def _():
        xs_ref[n] = x_ref[...]

    shift = mod_ref[:, 0:D_MODEL]
    scale = mod_ref[:, D_MODEL:2 * D_MODEL]
    gate = mod_ref[:, 2 * D_MODEL:3 * D_MODEL]
    pre = gpre_ref[...] * (1.0 + scale)
    xs = [xs_ref[n, t] for t in range(S)]
    hs = []
    for t in range(S):
        r = lax.rsqrt(jnp.mean(xs[t] * xs[t], axis=-1, keepdims=True) + RMS_EPS)
        hs.append(((xs[t] * r) * pre + shift).astype(BF16))
    hb = jnp.concatenate(hs, axis=0)

    pa = jnp.dot(hb, win_ref[:, COL_A:COL_A + 3 * W_A], preferred_element_type=F32)
    a = pa[:, 0:W_A] * jax.nn.sigmoid(pa[:, W_A:2 * W_A])
    a_z = pa[:, 2 * W_A:3 * W_A]
    n_hist = CONV_A - 1
    a_ext = lambda i: ha_ref[i] if i < n_hist else a[(i - n_hist) * N:(i - n_hist + 1) * N]
    for k in range(n_hist):
        ca_ref[k] = a_ext(k + S)
    for t in range(S):
        acc = jnp.broadcast_to(acb_ref[...], (N, W_A))
        for k in range(CONV_A):
            acc = acc + a_ext(t + k) * acw_ref[k:k + 1, :]
        ln = _layernorm(acc, alng_ref[...], alnb_ref[...])
        y_a = _silu(ln) * _silu(a_z[t * N:(t + 1) * N])
        mix_ref[t * N:(t + 1) * N, 0:W_A] = y_a.astype(BF16)

    pb = jnp.dot(hb, win_ref[:, COL_B:COL_B + 3 * W_B], preferred_element_type=F32)
    v_n = _layernorm(pb[:, W_B:2 * W_B], blng_ref[...], blnb_ref[...])
    vs = [v_n[t * N:(t + 1) * N] for t in range(S)]
    for t in range(S):
        v_ref[t] = vs[t]
        s = jnp.broadcast_to(bss_ref[t:t + 1, :], (N, W_B))
        for u in range(t + 1):
            s = s + vs[u] * wss_ref[t, u:u + 1, :]
        rows = slice(t * N, (t + 1) * N)
        y_b = pb[rows, 0:W_B] * s * _silu(pb[rows, 2 * W_B:3 * W_B])
        mix_ref[rows, W_A:W_A + W_B] = y_b.astype(BF16)

    pc = jnp.dot(hb, win_ref[:, COL_C:COL_C + 4 * W_C], preferred_element_type=F32)
    cx = pc[:, W_C:2 * W_C] * pc[:, 2 * W_C:3 * W_C]
    n_hc = CONV_C - 1
    c_ext = lambda i: hc_ref[i] if i < n_hc else cx[(i - n_hc) * N:(i - n_hc + 1) * N]
    for k in range(n_hc):
        cc_ref[k] = c_ext(k + S)
    for t in range(S):
        acc = c_ext(t) * ccw_ref[0:1, :]
        for k in range(1, CONV_C):
            acc = acc + c_ext(t + k) * ccw_ref[k:k + 1, :]
        rows = slice(t * N, (t + 1) * N)
        y_c = pc[rows, 0:W_C] * acc * _silu(pc[rows, 3 * W_C:4 * W_C])
        mix_ref[rows, W_A + W_B:D_MIX] = y_c.astype(BF16)

    mix = jnp.dot(mix_ref[...], wout_ref[...], preferred_element_type=F32)
    r2 = lax.rsqrt(jnp.mean(mix * mix, axis=-1, keepdims=True) + RMS_EPS)
    post = gate * gpost_ref[...]
    ys = [xs[t] + (mix[t * N:(t + 1) * N] * r2[t * N:(t + 1) * N]) * post for t in range(S)]
    for t in range(S):
        xs_ref[n, t] = ys[t]

    @pl.when(l == DEPTH - 1)
    def _():
        for t in range(S):
            y_ref[t] = ys[t]


def _sample_layers(x_t, mod_all, g_pre, g_post, w_in_b, w_out_b, a_conv_w, a_conv_b,
                   a_ln_g, a_ln_b, b_ln_g, b_ln_b, wss, bss, c_conv_w, ha_t, hc_t):
    S, N = DEC_SEQ, SAMPLE_N
    nsplit = DEC_BATCH // N
    row = lambda w: pl.BlockSpec((None, 1, w), lambda l, n: (l, 0, 0))
    lay3 = lambda a, b: pl.BlockSpec((None, a, b), lambda l, n: (l, 0, 0))
    lay4 = lambda a, b, c: pl.BlockSpec((None, a, b, c), lambda l, n: (l, 0, 0, 0))
    seq4 = lambda a, c: pl.BlockSpec((None, a, N, c), lambda l, n: (l, 0, n, 0))
    return pl.pallas_call(
        _sample_kernel,
        grid=(DEPTH, nsplit),
        in_specs=[
            pl.BlockSpec((S, N, D_MODEL), lambda l, n: (0, jnp.where(l == 0, n, 0), 0)),
            pl.BlockSpec((None, N, 3 * D_MODEL), lambda l, n: (l, n, 0)),
            row(D_MODEL), row(D_MODEL),
            *[pl.BlockSpec(memory_space=pl.ANY)] * (2 * DEPTH),
            lay3(CONV_A, W_A),
            row(W_A), row(W_A), row(W_A), row(W_B), row(W_B),
            lay4(S, S, W_B), lay3(S, W_B), lay3(CONV_C, W_C),
            seq4(CONV_A - 1, W_A), seq4(CONV_C - 1, W_C),
        ],
        out_specs=[
            pl.BlockSpec((S, N, D_MODEL), lambda l, n: (0, jnp.where(l == DEPTH - 1, n, 0), 0)),
            seq4(CONV_A - 1, W_A), seq4(CONV_C - 1, W_C), seq4(S, W_B),
        ],
        out_shape=[
            jax.ShapeDtypeStruct((S, DEC_BATCH, D_MODEL), F32),
            jax.ShapeDtypeStruct((DEPTH, CONV_A - 1, DEC_BATCH, W_A), F32),
            jax.ShapeDtypeStruct((DEPTH, CONV_C - 1, DEC_BATCH, W_C), F32),
            jax.ShapeDtypeStruct((DEPTH, S, DEC_BATCH, W_B), F32),
        ],
        scratch_shapes=[
            pltpu.VMEM((2, D_MODEL, P_IN), BF16),
            pltpu.VMEM((2, D_MIX, D_MODEL), BF16),
            pltpu.SemaphoreType.DMA((2, 2)),
            pltpu.VMEM((nsplit, S, N, D_MODEL), F32),
            pltpu.VMEM((S * N, D_MIX), BF16),
        ],
        compiler_params=pltpu.CompilerParams(
            dimension_semantics=("arbitrary", "arbitrary"), vmem_limit_bytes=VMEM_LIMIT),
        name="sample_layers",
    )(x_t, mod_all, g_pre, g_post, *w_in_b, *w_out_b, a_conv_w, a_conv_b,
      a_ln_g, a_ln_b, b_ln_g, b_ln_b, wss, bss, c_conv_w, ha_t, hc_t)


def kernel(x_prompt, x_sample, c_prompt, c_sample, state_conv_a, state_conv_c, w_mod, b_mod,
           g_pre, g_post, w_in, w_out, a_conv_w, a_conv_b, a_ln_g, a_ln_b, b_ln_g, b_ln_b,
           b_ws, b_bs, c_conv_w):
    S = DEC_SEQ
    r3 = lambda p: p.reshape(DEPTH, 1, p.shape[-1])
    g_pre3, g_post3 = r3(g_pre), r3(g_post)
    acb3, alng3, alnb3, blng3, blnb3 = r3(a_conv_b), r3(a_ln_g), r3(a_ln_b), r3(b_ln_g), r3(b_ln_b)

    c_all = jnp.concatenate([c_sample, c_prompt], axis=0)
    mod_all, w_in_b0, w_out_b0 = _mod_call(c_all, w_mod, b_mod, w_in, w_out)
    w_in_b, w_out_b = [w_in_b0], [w_out_b0]
    mod4 = mod_all.reshape(DEPTH, N_ALL, 3, D_MODEL)

    bsb = jnp.repeat(jnp.swapaxes(b_bs, 1, 2), HEAD_W, axis=2)
    wss = jnp.repeat(jnp.transpose(b_ws[:, :, :S, :S], (0, 2, 3, 1)), HEAD_W, axis=3)
    bss = bsb[:, :S, :]

    xp = x_prompt.reshape(BATCH * SEQ, D_MODEL)
    ca_p, cc_p = [], []
    for l in range(DEPTH):
        outs = _prompt_layer(l, xp, mod4, g_pre3, g_post3, w_in_b[l], w_out_b[l], w_in, w_out, a_conv_w, acb3,
                             alng3, alnb3, blng3, blnb3, b_ws, bsb, c_conv_w)
        xp, ca, cc = outs[:3]
        ca_p.append(ca)
        cc_p.append(cc)
        if l + 1 < DEPTH:
            w_in_b.append(outs[3])
            w_out_b.append(outs[4])

    x_t = jnp.swapaxes(x_sample, 0, 1)
    ha_t = jnp.swapaxes(state_conv_a, 1, 2)
    hc_t = jnp.swapaxes(state_conv_c, 1, 2)
    y_t, ca_t, cc_t, v_t = _sample_layers(
        x_t, mod_all, g_pre3, g_post3, w_in_b, w_out_b, a_conv_w, acb3,
        alng3, alnb3, blng3, blnb3, wss, bss, c_conv_w, ha_t, hc_t)

    return (xp.reshape(BATCH, SEQ, D_MODEL),
            jnp.swapaxes(y_t, 0, 1),
            jnp.stack(ca_p), jnp.stack(cc_p),
            jnp.swapaxes(ca_t, 1, 2), jnp.swapaxes(cc_t, 1, 2), jnp.swapaxes(v_t, 1, 2))
```

```python
import functools

import jax
import jax.numpy as jnp
from jax import lax
from jax.experimental import pallas as pl
from jax.experimental.pallas import tpu as pltpu

D_MODEL = 1024
BATCH = 8
SEQ = 2048
DEPTH = 4
DEC_BATCH = 128
DEC_SEQ = 4
W_A = 512
W_B = 512
W_C = 512
D_MIX = W_A + W_B + W_C
P_IN = 3 * W_A + 3 * W_B + 4 * W_C
CONV_A = 31
CONV_C = 3
CHUNK = 128
HEADS_B = 4
HEAD_W = W_B // HEADS_B
RMS_EPS = 1e-6
LN_EPS = 1e-5

LANES = 128
SUBLANES = 8
N_LANE_CHUNKS = W_A // LANES

COL_A = 0
COL_B = 3 * W_A
COL_C = 3 * W_A + 3 * W_B

TILE_T = 512
HIST_A = 32
HIST_C = 8
CONV_ROWS = 16
EDGE_ROWS = TILE_T // 4
N_ALL = DEC_BATCH + BATCH
MOD_COL_BLOCKS = 2
SAMPLE_N = 64

VMEM_LIMIT = 56 * 1024 * 1024

F32 = jnp.float32
BF16 = jnp.bfloat16


def _silu(x):
    return x * jax.nn.sigmoid(x)


def _layernorm(x, g, b):
    mu = jnp.mean(x, axis=-1, keepdims=True)
    xc = x - mu
    var = jnp.mean(xc * xc, axis=-1, keepdims=True)
    return xc * lax.rsqrt(var + LN_EPS) * g + b


def _mod_kernel(c_ref, w_ref, b_ref, win0_f32, wout0_f32, o_ref, win0_b, wout0_b):
    c = c_ref[...]
    s = _silu(c).astype(BF16)
    o_ref[...] = jnp.dot(s, w_ref[...].astype(BF16), preferred_element_type=F32) + b_ref[...]
    win0_b[...] = win0_f32[...].astype(BF16)
    wout0_b[...] = wout0_f32[...].astype(BF16)


def _mod_call(c_all, w_mod, b_mod, w_in, w_out):
    n_col = MOD_COL_BLOCKS
    tn = 3 * D_MODEL // n_col
    n_steps = DEPTH * n_col
    rows_in, rows_out = D_MODEL // n_steps, D_MIX // n_steps
    assert tn * n_col == 3 * D_MODEL and tn % LANES == 0
    assert rows_in * n_steps == D_MODEL and rows_out * n_steps == D_MIX
    assert rows_in % (2 * SUBLANES) == 0 and rows_out % (2 * SUBLANES) == 0
    return pl.pallas_call(
        _mod_kernel,
        grid=(DEPTH, n_col),
        in_specs=[
            pl.BlockSpec((N_ALL, D_MODEL), lambda l, n: (0, 0)),
            pl.BlockSpec((None, D_MODEL, tn), lambda l, n: (l, 0, n)),
            pl.BlockSpec((None, 1, tn), lambda l, n: (l, 0, n)),
            pl.BlockSpec((None, rows_in, P_IN), lambda l, n: (0, l * n_col + n, 0)),
            pl.BlockSpec((None, rows_out, D_MODEL), lambda l, n: (0, l * n_col + n, 0)),
        ],
        out_specs=[
            pl.BlockSpec((None, N_ALL, tn), lambda l, n: (l, 0, n)),
            pl.BlockSpec((rows_in, P_IN), lambda l, n: (l * n_col + n, 0)),
            pl.BlockSpec((rows_out, D_MODEL), lambda l, n: (l * n_col + n, 0)),
        ],
        out_shape=[
            jax.ShapeDtypeStruct((DEPTH, N_ALL, 3 * D_MODEL), F32),
            jax.ShapeDtypeStruct((D_MODEL, P_IN), BF16),
            jax.ShapeDtypeStruct((D_MIX, D_MODEL), BF16),
        ],
        compiler_params=pltpu.CompilerParams(
            dimension_semantics=("arbitrary", "arbitrary"), vmem_limit_bytes=VMEM_LIMIT),
        name="mod",
    )(c_all, w_mod, b_mod.reshape(DEPTH, 1, 3 * D_MODEL), w_in, w_out)


def _prompt_kernel(cast_next, x_ref, mod_ref, gpre_ref, gpost_ref, win_hbm, wout_hbm,
                   acw_ref, acb_ref, alng_ref, alnb_ref, blng_ref, blnb_ref,
                   ws_ref, bsb_ref, ccw_ref, *rest):
    if cast_next:
        nwin_f32, nwout_f32, y_ref, ca_ref, cc_ref, nwin_b, nwout_b, *scratch = rest
    else:
        y_ref, ca_ref, cc_ref, *scratch = rest
    win_ref, wout_ref, w_sem, hb_ref, p_ref, aext_ref, cext_ref, aconv_ref, vb_ref, mix_ref = scratch
    T = TILE_T
    j = pl.program_id(1)
    off_a = HIST_A - (CONV_A - 1)
    off_c = HIST_C - (CONV_C - 1)

    @pl.when((pl.program_id(0) == 0) & (j == 0))
    def _():
        cp_in = pltpu.make_async_copy(win_hbm, win_ref, w_sem.at[0])
        cp_out = pltpu.make_async_copy(wout_hbm, wout_ref, w_sem.at[1])
        cp_in.start()
        cp_out.start()
        cp_in.wait()
        cp_out.wait()

    @pl.when(j == 0)
    def _():
        aext_ref[:, 0:HIST_A, :] = jnp.zeros((N_LANE_CHUNKS, HIST_A, LANES), F32)
        cext_ref[:, 0:HIST_C, :] = jnp.zeros((N_LANE_CHUNKS, HIST_C, LANES), F32)

    shift = mod_ref[0:1, :]
    scale = mod_ref[1:2, :]
    gate = mod_ref[2:3, :]
    pre = gpre_ref[...] * (1.0 + scale)
    for h0 in range(0, T, EDGE_ROWS):
        hr = slice(h0, h0 + EDGE_ROWS)
        x = x_ref[hr, :]
        r = lax.rsqrt(jnp.mean(x * x, axis=-1, keepdims=True) + RMS_EPS)
        hb_ref[hr, :] = ((x * r) * pre + shift).astype(BF16)

    def proj(col):
        p_ref[:, col:col + W_A] = jnp.dot(hb_ref[...], win_ref[:, col:col + W_A], preferred_element_type=F32)

    def pcol(col):
        return p_ref[:, col:col + W_A]

    def conv_a_chunk(c):
        lanes = slice(c * LANES, (c + 1) * LANES)
        for r0 in range(0, T, CONV_ROWS):
            acc = jnp.broadcast_to(acb_ref[:, lanes], (CONV_ROWS, LANES))
            for k in range(CONV_A):
                s0 = r0 + off_a + k
                acc = acc + aext_ref[c, s0:s0 + CONV_ROWS, :] * acw_ref[k:k + 1, lanes]
            aconv_ref[r0:r0 + CONV_ROWS, lanes] = acc

    for h0 in range(0, T, EDGE_ROWS):
        hr = slice(h0, h0 + EDGE_ROWS)
        for col in (COL_A, COL_A + W_A):
            p_ref[hr, col:col + W_A] = jnp.dot(hb_ref[hr, :], win_ref[:, col:col + W_A], preferred_element_type=F32)
    a = pcol(COL_A) * jax.nn.sigmoid(pcol(COL_A + W_A))
    for c in range(N_LANE_CHUNKS):
        aext_ref[c, HIST_A:HIST_A + T, :] = a[:, c * LANES:(c + 1) * LANES]
    proj(COL_A + 2 * W_A)
    conv_a_chunk(0)

    proj(COL_B + W_B)
    proj(COL_B)
    proj(COL_B + 2 * W_B)
    vb_ref[...] = _layernorm(pcol(COL_B + W_B), blng_ref[...], blnb_ref[...]).astype(BF16)
    conv_a_chunk(1)
    row_i = lax.broadcasted_iota(jnp.int32, (CHUNK, CHUNK), 0)
    col_i = lax.broadcasted_iota(jnp.int32, (CHUNK, CHUNK), 1)
    wm = [jnp.where(row_i >= col_i, ws_ref[hd], 0.0).astype(BF16) for hd in range(HEADS_B)]
    col_s = COL_B + W_B
    for ch in range(T // CHUNK):
        rows = slice(ch * CHUNK, (ch + 1) * CHUNK)
        for hd in range(HEADS_B):
            hcols = slice(hd * HEAD_W, (hd + 1) * HEAD_W)
            p_ref[rows, col_s + hd * HEAD_W:col_s + (hd + 1) * HEAD_W] = (
                jnp.dot(wm[hd], vb_ref[rows, hcols], preferred_element_type=F32) + bsb_ref[:, hcols])
    y_b = pcol(COL_B) * pcol(col_s) * _silu(pcol(COL_B + 2 * W_B))
    mix_ref[:, W_A:W_A + W_B] = y_b.astype(BF16)

    proj(COL_C + W_C)
    proj(COL_C + 2 * W_C)
    conv_a_chunk(2)
    cx = pcol(COL_C + W_C) * pcol(COL_C + 2 * W_C)
    for c in range(N_LANE_CHUNKS):
        cext_ref[c, HIST_C:HIST_C + T, :] = cx[:, c * LANES:(c + 1) * LANES]
    proj(COL_C)
    proj(COL_C + 3 * W_C)
    conv_a_chunk(3)
    for c in range(N_LANE_CHUNKS):
        lanes = slice(c * LANES, (c + 1) * LANES)
        acc = cext_ref[c, off_c:off_c + T, :] * ccw_ref[0:1, lanes]
        for k in range(1, CONV_C):
            acc = acc + cext_ref[c, off_c + k:off_c + k + T, :] * ccw_ref[k:k + 1, lanes]
        y_c = p_ref[:, COL_C + c * LANES:COL_C + (c + 1) * LANES] * acc * _silu(
            p_ref[:, COL_C + 3 * W_C + c * LANES:COL_C + 3 * W_C + (c + 1) * LANES])
        mix_ref[:, W_A + W_B + c * LANES:W_A + W_B + (c + 1) * LANES] = y_c.astype(BF16)

    if cast_next:
        nwin_b[...] = nwin_f32[...].astype(BF16)
        nwout_b[...] = nwout_f32[...].astype(BF16)

    out_bc = jnp.dot(mix_ref[:, W_A:D_MIX], wout_ref[W_A:D_MIX, :], preferred_element_type=F32)
    y_a = _silu(_layernorm(aconv_ref[...], alng_ref[...], alnb_ref[...])) * _silu(pcol(COL_A + 2 * W_A))
    mix_ref[:, 0:W_A] = y_a.astype(BF16)
    post = gate * gpost_ref[...]
    for h0 in range(0, T, EDGE_ROWS):
        hr = slice(h0, h0 + EDGE_ROWS)
        mix = out_bc[hr] + jnp.dot(mix_ref[hr, 0:W_A], wout_ref[0:W_A, :], preferred_element_type=F32)
        r2 = lax.rsqrt(jnp.mean(mix * mix, axis=-1, keepdims=True) + RMS_EPS)
        y_ref[hr, :] = x_ref[hr, :] + (mix * r2) * post

    @pl.when(j == pl.num_programs(1) - 1)
    def _():
        for c in range(N_LANE_CHUNKS):
            ca_ref[:, c * LANES:(c + 1) * LANES] = aext_ref[c, T + off_a:T + HIST_A, :]
            cc_ref[:, c * LANES:(c + 1) * LANES] = cext_ref[c, T + off_c:T + HIST_C, :]

    for c in range(N_LANE_CHUNKS):
        aext_ref[c, 0:HIST_A, :] = aext_ref[c, T:T + HIST_A, :]
        cext_ref[c, 0:HIST_C, :] = cext_ref[c, T:T + HIST_C, :]


def _prompt_layer(l, x, mod4, g_pre, g_post, w_in_b, w_out_b, w_in, w_out, a_conv_w, a_conv_b,
                  a_ln_g, a_ln_b, b_ln_g, b_ln_b, b_ws, bsb, c_conv_w):
    nt = SEQ // TILE_T
    n_steps = BATCH * nt
    cast_next = l + 1 < DEPTH
    rows_in, rows_out = D_MODEL // n_steps, D_MIX // n_steps
    assert rows_in * n_steps == D_MODEL and rows_out * n_steps == D_MIX
    assert rows_in % (2 * SUBLANES) == 0 and rows_out % (2 * SUBLANES) == 0
    row = lambda w: pl.BlockSpec((None, 1, w), lambda b, j: (l, 0, 0))
    in_specs = [
        pl.BlockSpec((TILE_T, D_MODEL), lambda b, j: (b * nt + j, 0)),
        pl.BlockSpec((None, None, 3, D_MODEL), lambda b, j: (l, DEC_BATCH + b, 0, 0)),
        row(D_MODEL), row(D_MODEL),
        pl.BlockSpec(memory_space=pl.ANY),
        pl.BlockSpec(memory_space=pl.ANY),
        pl.BlockSpec((None, CONV_A, W_A), lambda b, j: (l, 0, 0)),
        row(W_A), row(W_A), row(W_A), row(W_B), row(W_B),
        pl.BlockSpec((None, HEADS_B, CHUNK, CHUNK), lambda b, j: (l, 0, 0, 0)),
        pl.BlockSpec((None, CHUNK, W_B), lambda b, j: (l, 0, 0)),
        pl.BlockSpec((None, CONV_C, W_C), lambda b, j: (l, 0, 0)),
    ]
    out_specs = [
        pl.BlockSpec((TILE_T, D_MODEL), lambda b, j: (b * nt + j, 0)),
        pl.BlockSpec((None, CONV_A - 1, W_A), lambda b, j: (b, 0, 0)),
        pl.BlockSpec((None, CONV_C - 1, W_C), lambda b, j: (b, 0, 0)),
    ]
    out_shape = [
        jax.ShapeDtypeStruct((BATCH * SEQ, D_MODEL), F32),
        jax.ShapeDtypeStruct((BATCH, CONV_A - 1, W_A), F32),
        jax.ShapeDtypeStruct((BATCH, CONV_C - 1, W_C), F32),
    ]
    args = [x, mod4, g_pre, g_post, w_in_b, w_out_b, a_conv_w, a_conv_b,
            a_ln_g, a_ln_b, b_ln_g, b_ln_b, b_ws, bsb, c_conv_w]
    if cast_next:
        in_specs += [
            pl.BlockSpec((None, rows_in, P_IN), lambda b, j: (l + 1, b * nt + j, 0)),
            pl.BlockSpec((None, rows_out, D_MODEL), lambda b, j: (l + 1, b * nt + j, 0)),
        ]
        out_specs += [
            pl.BlockSpec((rows_in, P_IN), lambda b, j: (b * nt + j, 0)),
            pl.BlockSpec((rows_out, D_MODEL), lambda b, j: (b * nt + j, 0)),
        ]
        out_shape += [
            jax.ShapeDtypeStruct((D_MODEL, P_IN), BF16),
            jax.ShapeDtypeStruct((D_MIX, D_MODEL), BF16),
        ]
        args += [w_in, w_out]
    return pl.pallas_call(
        functools.partial(_prompt_kernel, cast_next),
        grid=(BATCH, nt),
        in_specs=in_specs,
        out_specs=out_specs,
        out_shape=out_shape,
        scratch_shapes=[
            pltpu.VMEM((D_MODEL, P_IN), BF16),
            pltpu.VMEM((D_MIX, D_MODEL), BF16),
            pltpu.SemaphoreType.DMA((2,)),
            pltpu.VMEM((TILE_T, D_MODEL), BF16),
            pltpu.VMEM((TILE_T, P_IN), F32),
            pltpu.VMEM((N_LANE_CHUNKS, TILE_T + HIST_A, LANES), F32),
            pltpu.VMEM((N_LANE_CHUNKS, TILE_T + HIST_C, LANES), F32),
            pltpu.VMEM((TILE_T, W_A), F32),
            pltpu.VMEM((TILE_T, W_B), BF16),
            pltpu.VMEM((TILE_T, D_MIX), BF16),
        ],
        compiler_params=pltpu.CompilerParams(
            dimension_semantics=("arbitrary", "arbitrary"), vmem_limit_bytes=VMEM_LIMIT),
        name=f"prompt_layer{l}",
    )(*args)


def _sample_kernel(x_ref, mod_ref, gpre_ref, gpost_ref, *rest):
    w_hbm, rest = rest[:2 * DEPTH], rest[2 * DEPTH:]
    (acw_ref, acb_ref, alng_ref, alnb_ref, blng_ref, blnb_ref, wss_ref, bss_ref, ccw_ref, ha_ref, hc_ref,
     y_ref, ca_ref, cc_ref, v_ref, win_ring, wout_ring, w_sem, xs_ref, mix_ref) = rest
    S, N = DEC_SEQ, SAMPLE_N
    l = pl.program_id(0)
    n = pl.program_id(1)
    slot = l % 2

    def w_copies(layer, to_slot):
        return (pltpu.make_async_copy(w_hbm[layer], win_ring.at[to_slot], w_sem.at[0, to_slot]),
                pltpu.make_async_copy(w_hbm[DEPTH + layer], wout_ring.at[to_slot], w_sem.at[1, to_slot]))

    @pl.when(n == 0)
    def _():
        @pl.when(l == 0)
        def _():
            for cp in w_copies(0, 0):
                cp.start()

        for layer in range(1, DEPTH):
            @pl.when(l + 1 == layer)
            def _():
                for cp in w_copies(layer, layer % 2):
                    cp.start()

        for cp in w_copies(0, slot):
            cp.wait()

    win_ref = win_ring.at[slot]
    wout_ref = wout_ring.at[slot]

    @pl.when(l == 0)
    def _():
        xs_ref[n] = x_ref[...]

    shift = mod_ref[:, 0:D_MODEL]
    scale = mod_ref[:, D_MODEL:2 * D_MODEL]
    gate = mod_ref[:, 2 * D_MODEL:3 * D_MODEL]
    pre = gpre_ref[...] * (1.0 + scale)
    xs = [xs_ref[n, t] for t in range(S)]
    hs = []
    for t in range(S):
        r = lax.rsqrt(jnp.mean(xs[t] * xs[t], axis=-1, keepdims=True) + RMS_EPS)
        hs.append(((xs[t] * r) * pre + shift).astype(BF16))
    hb = jnp.concatenate(hs, axis=0)

    pa = jnp.dot(hb, win_ref[:, COL_A:COL_A + 3 * W_A], preferred_element_type=F32)
    a = pa[:, 0:W_A] * jax.nn.sigmoid(pa[:, W_A:2 * W_A])
    a_z = pa[:, 2 * W_A:3 * W_A]
    n_hist = CONV_A - 1
    a_ext = lambda i: ha_ref[i] if i < n_hist else a[(i - n_hist) * N:(i - n_hist + 1) * N]
    for k in range(n_hist):
        ca_ref[k] = a_ext(k + S)
    for t in range(S):
        acc = jnp.broadcast_to(acb_ref[...], (N, W_A))
        for k in range(CONV_A):
            acc = acc + a_ext(t + k) * acw_ref[k:k + 1, :]
        ln = _layernorm(acc, alng_ref[...], alnb_ref[...])
        y_a = _silu(ln) * _silu(a_z[t * N:(t + 1) * N])
        mix_ref[t * N:(t + 1) * N, 0:W_A] = y_a.astype(BF16)

    pb = jnp.dot(hb, win_ref[:, COL_B:COL_B + 3 * W_B], preferred_element_type=F32)
    v_n = _layernorm(pb[:, W_B:2 * W_B], blng_ref[...], blnb_ref[...])
    vs = [v_n[t * N:(t + 1) * N] for t in range(S)]
    for t in range(S):
        v_ref[t] = vs[t]
        s = jnp.broadcast_to(bss_ref[t:t + 1, :], (N, W_B))
        for u in range(t + 1):
            s = s + vs[u] * wss_ref[t, u:u + 1, :]
        rows = slice(t * N, (t + 1) * N)
        y_b = pb[rows, 0:W_B] * s * _silu(pb[rows, 2 * W_B:3 * W_B])
        mix_ref[rows, W_A:W_A + W_B] = y_b.astype(BF16)

    pc = jnp.dot(hb, win_ref[:, COL_C:COL_C + 4 * W_C], preferred_element_type=F32)
    cx = pc[:, W_C:2 * W_C] * pc[:, 2 * W_C:3 * W_C]
    n_hc = CONV_C - 1
    c_ext = lambda i: hc_ref[i] if i < n_hc else cx[(i - n_hc) * N:(i - n_hc + 1) * N]
    for k in range(n_hc):
        cc_ref[k] = c_ext(k + S)
    for t in range(S):
        acc = c_ext(t) * ccw_ref[0:1, :]
        for k in range(1, CONV_C):
            acc = acc + c_ext(t + k) * ccw_ref[k:k + 1, :]
        rows = slice(t * N, (t + 1) * N)
        y_c = pc[rows, 0:W_C] * acc * _silu(pc[rows, 3 * W_C:4 * W_C])
        mix_ref[rows, W_A + W_B:D_MIX] = y_c.astype(BF16)

    mix = jnp.dot(mix_ref[...], wout_ref[...], preferred_element_type=F32)
    r2 = lax.rsqrt(jnp.mean(mix * mix, axis=-1, keepdims=True) + RMS_EPS)
    post = gate * gpost_ref[...]
    ys = [xs[t] + (mix[t * N:(t + 1) * N] * r2[t * N:(t + 1) * N]) * post for t in range(S)]
    for t in range(S):
        xs_ref[n, t] = ys[t]

    @pl.when(l == DEPTH - 1)
    def _():
        for t in range(S):
            y_ref[t] = ys[t]


def _sample_layers(x_t, mod_all, g_pre, g_post, w_in_b, w_out_b, a_conv_w, a_conv_b,
                   a_ln_g, a_ln_b, b_ln_g, b_ln_b, wss, bss, c_conv_w, ha_t, hc_t):
    S, N = DEC_SEQ, SAMPLE_N
    nsplit = DEC_BATCH // N
    row = lambda w: pl.BlockSpec((None, 1, w), lambda l, n: (l, 0, 0))
    lay3 = lambda a, b: pl.BlockSpec((None, a, b), lambda l, n: (l, 0, 0))
    lay4 = lambda a, b, c: pl.BlockSpec((None, a, b, c), lambda l, n: (l, 0, 0, 0))
    seq4 = lambda a, c: pl.BlockSpec((None, a, N, c), lambda l, n: (l, 0, n, 0))
    return pl.pallas_call(
        _sample_kernel,
        grid=(DEPTH, nsplit),
        in_specs=[
            pl.BlockSpec((S, N, D_MODEL), lambda l, n: (0, jnp.where(l == 0, n, 0), 0)),
            pl.BlockSpec((None, N, 3 * D_MODEL), lambda l, n: (l, n, 0)),
            row(D_MODEL), row(D_MODEL),
            *[pl.BlockSpec(memory_space=pl.ANY)] * (2 * DEPTH),
            lay3(CONV_A, W_A),
            row(W_A), row(W_A), row(W_A), row(W_B), row(W_B),
            lay4(S, S, W_B), lay3(S, W_B), lay3(CONV_C, W_C),
            seq4(CONV_A - 1, W_A), seq4(CONV_C - 1, W_C),
        ],
        out_specs=[
            pl.BlockSpec((S, N, D_MODEL), lambda l, n: (0, jnp.where(l == DEPTH - 1, n, 0), 0)),
            seq4(CONV_A - 1, W_A), seq4(CONV_C - 1, W_C), seq4(S, W_B),
        ],
        out_shape=[
            jax.ShapeDtypeStruct((S, DEC_BATCH, D_MODEL), F32),
            jax.ShapeDtypeStruct((DEPTH, CONV_A - 1, DEC_BATCH, W_A), F32),
            jax.ShapeDtypeStruct((DEPTH, CONV_C - 1, DEC_BATCH, W_C), F32),
            jax.ShapeDtypeStruct((DEPTH, S, DEC_BATCH, W_B), F32),
        ],
        scratch_shapes=[
            pltpu.VMEM((2, D_MODEL, P_IN), BF16),
            pltpu.VMEM((2, D_MIX, D_MODEL), BF16),
            pltpu.SemaphoreType.DMA((2, 2)),
            pltpu.VMEM((nsplit, S, N, D_MODEL), F32),
            pltpu.VMEM((S * N, D_MIX), BF16),
        ],
        compiler_params=pltpu.CompilerParams(
            dimension_semantics=("arbitrary", "arbitrary"), vmem_limit_bytes=VMEM_LIMIT),
        name="sample_layers",
    )(x_t, mod_all, g_pre, g_post, *w_in_b, *w_out_b, a_conv_w, a_conv_b,
      a_ln_g, a_ln_b, b_ln_g, b_ln_b, wss, bss, c_conv_w, ha_t, hc_t)


def kernel(x_prompt, x_sample, c_prompt, c_sample, state_conv_a, state_conv_c, w_mod, b_mod,
           g_pre, g_post, w_in, w_out, a_conv_w, a_conv_b, a_ln_g, a_ln_b, b_ln_g, b_ln_b,
           b_ws, b_bs, c_conv_w):
    S = DEC_SEQ
    r3 = lambda p: p.reshape(DEPTH, 1, p.shape[-1])
    g_pre3, g_post3 = r3(g_pre), r3(g_post)
    acb3, alng3, alnb3, blng3, blnb3 = r3(a_conv_b), r3(a_ln_g), r3(a_ln_b), r3(b_ln_g), r3(b_ln_b)

    c_all = jnp.concatenate([c_sample, c_prompt], axis=0)
    mod_all, w_in_b0, w_out_b0 = _mod_call(c_all, w_mod, b_mod, w_in, w_out)
    w_in_b, w_out_b = [w_in_b0], [w_out_b0]
    mod4 = mod_all.reshape(DEPTH, N_ALL, 3, D_MODEL)

    bsb = jnp.repeat(jnp.swapaxes(b_bs, 1, 2), HEAD_W, axis=2)
    wss = jnp.repeat(jnp.transpose(b_ws[:, :, :S, :S], (0, 2, 3, 1)), HEAD_W, axis=3)
    bss = bsb[:, :S, :]

    xp = x_prompt.reshape(BATCH * SEQ, D_MODEL)
    ca_p, cc_p = [], []
    for l in range(DEPTH):
        outs = _prompt_layer(l, xp, mod4, g_pre3, g_post3, w_in_b[l], w_out_b[l], w_in, w_out, a_conv_w, acb3,
                             alng3, alnb3, blng3, blnb3, b_ws, bsb, c_conv_w)
        xp, ca, cc = outs[:3]
        ca_p.append(ca)
        cc_p.append(cc)
        if l + 1 < DEPTH:
            w_in_b.append(outs[3])
            w_out_b.append(outs[4])

    x_t = jnp.swapaxes(x_sample, 0, 1)
    ha_t = jnp.swapaxes(state_conv_a, 1, 2)
    hc_t = jnp.swapaxes(state_conv_c, 1, 2)
    y_t, ca_t, cc_t, v_t = _sample_layers(
        x_t, mod_all, g_pre3, g_post3, w_in_b, w_out_b, a_conv_w, acb3,
        alng3, alnb3, blng3, blnb3, wss, bss, c_conv_w, ha_t, hc_t)

    return (xp.reshape(BATCH, SEQ, D_MODEL),
            jnp.swapaxes(y_t, 0, 1),
            jnp.stack(ca_p), jnp.stack(cc_p),
            jnp.swapaxes(ca_t, 1, 2), jnp.swapaxes(cc_t, 1, 2), jnp.swapaxes(v_t, 1, 2))
```

```python
import functools

import jax
import jax.numpy as jnp
from jax import lax
from jax.experimental import pallas as pl
from jax.experimental.pallas import tpu as pltpu

D_MODEL = 1024
BATCH = 8
SEQ = 2048
DEPTH = 4
DEC_BATCH = 128
DEC_SEQ = 4
W_A = 512
W_B = 512
W_C = 512
D_MIX = W_A + W_B + W_C
P_IN = 3 * W_A + 3 * W_B + 4 * W_C
CONV_A = 31
CONV_C = 3
CHUNK = 128
HEADS_B = 4
HEAD_W = W_B // HEADS_B
RMS_EPS = 1e-6
LN_EPS = 1e-5

LANES = 128
SUBLANES = 8
N_LANE_CHUNKS = W_A // LANES

COL_A = 0
COL_B = 3 * W_A
COL_C = 3 * W_A + 3 * W_B

TILE_T = 512
HIST_A = 32
HIST_C = 8
CONV_ROWS = 16
EDGE_ROWS = TILE_T // 4
N_ALL = DEC_BATCH + BATCH
MOD_COL_BLOCKS = 2
SAMPLE_N = 64

VMEM_LIMIT = 56 * 1024 * 1024

F32 = jnp.float32
BF16 = jnp.bfloat16


def _silu(x):
    return x * jax.nn.sigmoid(x)


def _layernorm(x, g, b):
    mu = jnp.mean(x, axis=-1, keepdims=True)
    xc = x - mu
    var = jnp.mean(xc * xc, axis=-1, keepdims=True)
    return xc * lax.rsqrt(var + LN_EPS) * g + b


def _mod_kernel(c_ref, w_ref, b_ref, win0_f32, wout0_f32, o_ref, win0_b, wout0_b):
    c = c_ref[...]
    s = _silu(c).astype(BF16)
    o_ref[...] = jnp.dot(s, w_ref[...].astype(BF16), preferred_element_type=F32) + b_ref[...]
    win0_b[...] = win0_f32[...].astype(BF16)
    wout0_b[...] = wout0_f32[...].astype(BF16)


def _mod_call(c_all, w_mod, b_mod, w_in, w_out):
    n_col = MOD_COL_BLOCKS
    tn = 3 * D_MODEL // n_col
    n_steps = DEPTH * n_col
    rows_in, rows_out = D_MODEL // n_steps, D_MIX // n_steps
    assert tn * n_col == 3 * D_MODEL and tn % LANES == 0
    assert rows_in * n_steps == D_MODEL and rows_out * n_steps == D_MIX
    assert rows_in % (2 * SUBLANES) == 0 and rows_out % (2 * SUBLANES) == 0
    return pl.pallas_call(
        _mod_kernel,
        grid=(DEPTH, n_col),
        in_specs=[
            pl.BlockSpec((N_ALL, D_MODEL), lambda l, n: (0, 0)),
            pl.BlockSpec((None, D_MODEL, tn), lambda l, n: (l, 0, n)),
            pl.BlockSpec((None, 1, tn), lambda l, n: (l, 0, n)),
            pl.BlockSpec((None, rows_in, P_IN), lambda l, n: (0, l * n_col + n, 0)),
            pl.BlockSpec((None, rows_out, D_MODEL), lambda l, n: (0, l * n_col + n, 0)),
        ],
        out_specs=[
            pl.BlockSpec((None, N_ALL, tn), lambda l, n: (l, 0, n)),
            pl.BlockSpec((rows_in, P_IN), lambda l, n: (l * n_col + n, 0)),
            pl.BlockSpec((rows_out, D_MODEL), lambda l, n: (l * n_col + n, 0)),
        ],
        out_shape=[
            jax.ShapeDtypeStruct((DEPTH, N_ALL, 3 * D_MODEL), F32),
            jax.ShapeDtypeStruct((D_MODEL, P_IN), BF16),
            jax.ShapeDtypeStruct((D_MIX, D_MODEL), BF16),
        ],
        compiler_params=pltpu.CompilerParams(
            dimension_semantics=("arbitrary", "arbitrary"), vmem_limit_bytes=VMEM_LIMIT),
        name="mod",
    )(c_all, w_mod, b_mod.reshape(DEPTH, 1, 3 * D_MODEL), w_in, w_out)


def _prompt_kernel(cast_next, x_ref, mod_ref, gpre_ref, gpost_ref, win_hbm, wout_hbm,
                   acw_ref, acb_ref, alng_ref, alnb_ref, blng_ref, blnb_ref,
                   ws_ref, bsb_ref, ccw_ref, *rest):
    if cast_next:
        nwin_f32, nwout_f32, y_ref, ca_ref, cc_ref, nwin_b, nwout_b, *scratch = rest
        nwin_b[...] = nwin_f32[...].astype(BF16)
        nwout_b[...] = nwout_f32[...].astype(BF16)
    else:
        y_ref, ca_ref, cc_ref, *scratch = rest
    win_ref, wout_ref, w_sem, hb_ref, p_ref, aext_ref, cext_ref, aconv_ref, vb_ref, mix_ref = scratch
    T = TILE_T
    j = pl.program_id(1)
    off_a = HIST_A - (CONV_A - 1)
    off_c = HIST_C - (CONV_C - 1)

    @pl.when((pl.program_id(0) == 0) & (j == 0))
    def _():
        cp_in = pltpu.make_async_copy(win_hbm, win_ref, w_sem.at[0])
        cp_out = pltpu.make_async_copy(wout_hbm, wout_ref, w_sem.at[1])
        cp_in.start()
        cp_out.start()
        cp_in.wait()
        cp_out.wait()

    @pl.when(j == 0)
    def _():
        aext_ref[:, 0:HIST_A, :] = jnp.zeros((N_LANE_CHUNKS, HIST_A, LANES), F32)
        cext_ref[:, 0:HIST_C, :] = jnp.zeros((N_LANE_CHUNKS, HIST_C, LANES), F32)

    shift = mod_ref[0:1, :]
    scale = mod_ref[1:2, :]
    gate = mod_ref[2:3, :]
    pre = gpre_ref[...] * (1.0 + scale)
    for h0 in range(0, T, EDGE_ROWS):
        hr = slice(h0, h0 + EDGE_ROWS)
        x = x_ref[hr, :]
        r = lax.rsqrt(jnp.mean(x * x, axis=-1, keepdims=True) + RMS_EPS)
        hb_ref[hr, :] = ((x * r) * pre + shift).astype(BF16)

    def proj(col):
        p_ref[:, col:col + W_A] = jnp.dot(hb_ref[...], win_ref[:, col:col + W_A], preferred_element_type=F32)

    def pcol(col):
        return p_ref[:, col:col + W_A]

    def conv_a_chunk(c):
        lanes = slice(c * LANES, (c + 1) * LANES)
        for r0 in range(0, T, CONV_ROWS):
            acc = jnp.broadcast_to(acb_ref[:, lanes], (CONV_ROWS, LANES))
            for k in range(CONV_A):
                s0 = r0 + off_a + k
                acc = acc + aext_ref[c, s0:s0 + CONV_ROWS, :] * acw_ref[k:k + 1, lanes]
            aconv_ref[r0:r0 + CONV_ROWS, lanes] = acc

    for h0 in range(0, T, EDGE_ROWS):
        hr = slice(h0, h0 + EDGE_ROWS)
        for col in (COL_A, COL_A + W_A):
            p_ref[hr, col:col + W_A] = jnp.dot(hb_ref[hr, :], win_ref[:, col:col + W_A], preferred_element_type=F32)
    a = pcol(COL_A) * jax.nn.sigmoid(pcol(COL_A + W_A))
    for c in range(N_LANE_CHUNKS):
        aext_ref[c, HIST_A:HIST_A + T, :] = a[:, c * LANES:(c + 1) * LANES]
    proj(COL_A + 2 * W_A)
    conv_a_chunk(0)

    proj(COL_B + W_B)
    proj(COL_B)
    proj(COL_B + 2 * W_B)
    vb_ref[...] = _layernorm(pcol(COL_B + W_B), blng_ref[...], blnb_ref[...]).astype(BF16)
    conv_a_chunk(1)
    row_i = lax.broadcasted_iota(jnp.int32, (CHUNK, CHUNK), 0)
    col_i = lax.broadcasted_iota(jnp.int32, (CHUNK, CHUNK), 1)
    wm = [jnp.where(row_i >= col_i, ws_ref[hd], 0.0).astype(BF16) for hd in range(HEADS_B)]
    col_s = COL_B + W_B
    for ch in range(0, T // CHUNK, 2):
        rows0 = slice(ch * CHUNK, (ch + 1) * CHUNK)
        rows1 = slice((ch + 1) * CHUNK, (ch + 2) * CHUNK)
        for hd in range(HEADS_B):
            hcols = slice(hd * HEAD_W, (hd + 1) * HEAD_W)
            ocols = slice(col_s + hd * HEAD_W, col_s + (hd + 1) * HEAD_W)
            pair = jnp.concatenate([vb_ref[rows0, hcols], vb_ref[rows1, hcols]], axis=1)
            res = jnp.dot(wm[hd], pair, preferred_element_type=F32)
            p_ref[rows0, ocols] = res[:, 0:HEAD_W] + bsb_ref[:, hcols]
            p_ref[rows1, ocols] = res[:, HEAD_W:2 * HEAD_W] + bsb_ref[:, hcols]
    y_b = pcol(COL_B) * pcol(col_s) * _silu(pcol(COL_B + 2 * W_B))
    mix_ref[:, W_A:W_A + W_B] = y_b.astype(BF16)

    proj(COL_C + W_C)
    proj(COL_C + 2 * W_C)
    conv_a_chunk(2)
    cx = pcol(COL_C + W_C) * pcol(COL_C + 2 * W_C)
    for c in range(N_LANE_CHUNKS):
        cext_ref[c, HIST_C:HIST_C + T, :] = cx[:, c * LANES:(c + 1) * LANES]
    proj(COL_C)
    proj(COL_C + 3 * W_C)
    conv_a_chunk(3)
    for c in range(N_LANE_CHUNKS):
        lanes = slice(c * LANES, (c + 1) * LANES)
        acc = cext_ref[c, off_c:off_c + T, :] * ccw_ref[0:1, lanes]
        for k in range(1, CONV_C):
            acc = acc + cext_ref[c, off_c + k:off_c + k + T, :] * ccw_ref[k:k + 1, lanes]
        y_c = p_ref[:, COL_C + c * LANES:COL_C + (c + 1) * LANES] * acc * _silu(
            p_ref[:, COL_C + 3 * W_C + c * LANES:COL_C + 3 * W_C + (c + 1) * LANES])
        mix_ref[:, W_A + W_B + c * LANES:W_A + W_B + (c + 1) * LANES] = y_c.astype(BF16)

    out_bc = jnp.dot(mix_ref[:, W_A:D_MIX], wout_ref[W_A:D_MIX, :], preferred_element_type=F32)
    y_a = _silu(_layernorm(aconv_ref[...], alng_ref[...], alnb_ref[...])) * _silu(pcol(COL_A + 2 * W_A))
    mix_ref[:, 0:W_A] = y_a.astype(BF16)
    post = gate * gpost_ref[...]
    for h0 in range(0, T, EDGE_ROWS):
        hr = slice(h0, h0 + EDGE_ROWS)
        mix = out_bc[hr] + jnp.dot(mix_ref[hr, 0:W_A], wout_ref[0:W_A, :], preferred_element_type=F32)
        r2 = lax.rsqrt(jnp.mean(mix * mix, axis=-1, keepdims=True) + RMS_EPS)
        y_ref[hr, :] = x_ref[hr, :] + (mix * r2) * post

    @pl.when(j == pl.num_programs(1) - 1)
    def _():
        for c in range(N_LANE_CHUNKS):
            ca_ref[:, c * LANES:(c + 1) * LANES] = aext_ref[c, T + off_a:T + HIST_A, :]
            cc_ref[:, c * LANES:(c + 1) * LANES] = cext_ref[c, T + off_c:T + HIST_C, :]

    for c in range(N_LANE_CHUNKS):
        aext_ref[c, 0:HIST_A, :] = aext_ref[c, T:T + HIST_A, :]
        cext_ref[c, 0:HIST_C, :] = cext_ref[c, T:T + HIST_C, :]


def _prompt_layer(l, x, mod4, g_pre, g_post, w_in_b, w_out_b, w_in, w_out, a_conv_w, a_conv_b,
                  a_ln_g, a_ln_b, b_ln_g, b_ln_b, b_ws, bsb, c_conv_w):
    nt = SEQ // TILE_T
    n_steps = BATCH * nt
    cast_next = l + 1 < DEPTH
    rows_in, rows_out = D_MODEL // n_steps, D_MIX // n_steps
    assert rows_in * n_steps == D_MODEL and rows_out * n_steps == D_MIX
    assert rows_in % (2 * SUBLANES) == 0 and rows_out % (2 * SUBLANES) == 0
    row = lambda w: pl.BlockSpec((None, 1, w), lambda b, j: (l, 0, 0))
    in_specs = [
        pl.BlockSpec((TILE_T, D_MODEL), lambda b, j: (b * nt + j, 0)),
        pl.BlockSpec((None, None, 3, D_MODEL), lambda b, j: (l, DEC_BATCH + b, 0, 0)),
        row(D_MODEL), row(D_MODEL),
        pl.BlockSpec(memory_space=pl.ANY),
        pl.BlockSpec(memory_space=pl.ANY),
        pl.BlockSpec((None, CONV_A, W_A), lambda b, j: (l, 0, 0)),
        row(W_A), row(W_A), row(W_A), row(W_B), row(W_B),
        pl.BlockSpec((None, HEADS_B, CHUNK, CHUNK), lambda b, j: (l, 0, 0, 0)),
        pl.BlockSpec((None, CHUNK, W_B), lambda b, j: (l, 0, 0)),
        pl.BlockSpec((None, CONV_C, W_C), lambda b, j: (l, 0, 0)),
    ]
    out_specs = [
        pl.BlockSpec((TILE_T, D_MODEL), lambda b, j: (b * nt + j, 0)),
        pl.BlockSpec((None, CONV_A - 1, W_A), lambda b, j: (b, 0, 0)),
        pl.BlockSpec((None, CONV_C - 1, W_C), lambda b, j: (b, 0, 0)),
    ]
    out_shape = [
        jax.ShapeDtypeStruct((BATCH * SEQ, D_MODEL), F32),
        jax.ShapeDtypeStruct((BATCH, CONV_A - 1, W_A), F32),
        jax.ShapeDtypeStruct((BATCH, CONV_C - 1, W_C), F32),
    ]
    args = [x, mod4, g_pre, g_post, w_in_b, w_out_b, a_conv_w, a_conv_b,
            a_ln_g, a_ln_b, b_ln_g, b_ln_b, b_ws, bsb, c_conv_w]
    if cast_next:
        in_specs += [
            pl.BlockSpec((None, rows_in, P_IN), lambda b, j: (l + 1, b * nt + j, 0)),
            pl.BlockSpec((None, rows_out, D_MODEL), lambda b, j: (l + 1, b * nt + j, 0)),
        ]
        out_specs += [
            pl.BlockSpec((rows_in, P_IN), lambda b, j: (b * nt + j, 0)),
            pl.BlockSpec((rows_out, D_MODEL), lambda b, j: (b * nt + j, 0)),
        ]
        out_shape += [
            jax.ShapeDtypeStruct((D_MODEL, P_IN), BF16),
            jax.ShapeDtypeStruct((D_MIX, D_MODEL), BF16),
        ]
        args += [w_in, w_out]
    return pl.pallas_call(
        functools.partial(_prompt_kernel, cast_next),
        grid=(BATCH, nt),
        in_specs=in_specs,
        out_specs=out_specs,
        out_shape=out_shape,
        scratch_shapes=[
            pltpu.VMEM((D_MODEL, P_IN), BF16),
            pltpu.VMEM((D_MIX, D_MODEL), BF16),
            pltpu.SemaphoreType.DMA((2,)),
            pltpu.VMEM((TILE_T, D_MODEL), BF16),
            pltpu.VMEM((TILE_T, P_IN), F32),
            pltpu.VMEM((N_LANE_CHUNKS, TILE_T + HIST_A, LANES), F32),
            pltpu.VMEM((N_LANE_CHUNKS, TILE_T + HIST_C, LANES), F32),
            pltpu.VMEM((TILE_T, W_A), F32),
            pltpu.VMEM((TILE_T, W_B), BF16),
            pltpu.VMEM((TILE_T, D_MIX), BF16),
        ],
        compiler_params=pltpu.CompilerParams(
            dimension_semantics=("arbitrary", "arbitrary"), vmem_limit_bytes=VMEM_LIMIT),
        name=f"prompt_layer{l}",
    )(*args)


def _sample_kernel(x_ref, mod_ref, gpre_ref, gpost_ref, *rest):
    w_hbm, rest = rest[:2 * DEPTH], rest[2 * DEPTH:]
    (acw_ref, acb_ref, alng_ref, alnb_ref, blng_ref, blnb_ref, wss_ref, bss_ref, ccw_ref, ha_ref, hc_ref,
     y_ref, ca_ref, cc_ref, v_ref, win_ring, wout_ring, w_sem, xs_ref, mix_ref) = rest
    S, N = DEC_SEQ, SAMPLE_N
    l = pl.program_id(0)
    n = pl.program_id(1)
    slot = l % 2

    def w_copies(layer, to_slot):
        return (pltpu.make_async_copy(w_hbm[layer], win_ring.at[to_slot], w_sem.at[0, to_slot]),
                pltpu.make_async_copy(w_hbm[DEPTH + layer], wout_ring.at[to_slot], w_sem.at[1, to_slot]))

    @pl.when(n == 0)
    def _():
        @pl.when(l == 0)
        def _():
            for cp in w_copies(0, 0):
                cp.start()

        for layer in range(1, DEPTH):
            @pl.when(l + 1 == layer)
            def _():
                for cp in w_copies(layer, layer % 2):
                    cp.start()

        for cp in w_copies(0, slot):
            cp.wait()

    win_ref = win_ring.at[slot]
    wout_ref = wout_ring.at[slot]

    @pl.when(l == 0)
    def _():
        xs_ref[n] = x_ref[...]

    shift = mod_ref[:, 0:D_MODEL]
    scale = mod_ref[:, D_MODEL:2 * D_MODEL]
    gate = mod_ref[:, 2 * D_MODEL:3 * D_MODEL]
    pre = gpre_ref[...] * (1.0 + scale)
    xs = [xs_ref[n, t] for t in range(S)]
    hs = []
    for t in range(S):
        r = lax.rsqrt(jnp.mean(xs[t] * xs[t], axis=-1, keepdims=True) + RMS_EPS)
        hs.append(((xs[t] * r) * pre + shift).astype(BF16))
    hb = jnp.concatenate(hs, axis=0)

    pa = jnp.dot(hb, win_ref[:, COL_A:COL_A + 3 * W_A], preferred_element_type=F32)
    a = pa[:, 0:W_A] * jax.nn.sigmoid(pa[:, W_A:2 * W_A])
    a_z = pa[:, 2 * W_A:3 * W_A]
    n_hist = CONV_A - 1
    a_ext = lambda i: ha_ref[i] if i < n_hist else a[(i - n_hist) * N:(i - n_hist + 1) * N]
    for k in range(n_hist):
        ca_ref[k] = a_ext(k + S)
    for t in range(S):
        acc = jnp.broadcast_to(acb_ref[...], (N, W_A))
        for k in range(CONV_A):
            acc = acc + a_ext(t + k) * acw_ref[k:k + 1, :]
        ln = _layernorm(acc, alng_ref[...], alnb_ref[...])
        y_a = _silu(ln) * _silu(a_z[t * N:(t + 1) * N])
        mix_ref[t * N:(t + 1) * N, 0:W_A] = y_a.astype(BF16)

    pb = jnp.dot(hb, win_ref[:, COL_B:COL_B + 3 * W_B], preferred_element_type=F32)
    v_n = _layernorm(pb[:, W_B:2 * W_B], blng_ref[...], blnb_ref[...])
    vs = [v_n[t * N:(t + 1) * N] for t in range(S)]
    for t in range(S):
        v_ref[t] = vs[t]
        s = jnp.broadcast_to(bss_ref[t:t + 1, :], (N, W_B))
        for u in range(t + 1):
            s = s + vs[u] * wss_ref[t, u:u + 1, :]
        rows = slice(t * N, (t + 1) * N)
        y_b = pb[rows, 0:W_B] * s * _silu(pb[rows, 2 * W_B:3 * W_B])
        mix_ref[rows, W_A:W_A + W_B] = y_b.astype(BF16)

    pc = jnp.dot(hb, win_ref[:, COL_C:COL_C + 4 * W_C], preferred_element_type=F32)
    cx = pc[:, W_C:2 * W_C] * pc[:, 2 * W_C:3 * W_C]
    n_hc = CONV_C - 1
    c_ext = lambda i: hc_ref[i] if i < n_hc else cx[(i - n_hc) * N:(i - n_hc + 1) * N]
    for k in range(n_hc):
        cc_ref[k] = c_ext(k + S)
    for t in range(S):
        acc = c_ext(t) * ccw_ref[0:1, :]
        for k in range(1, CONV_C):
            acc = acc + c_ext(t + k) * ccw_ref[k:k + 1, :]
        rows = slice(t * N, (t + 1) * N)
        y_c = pc[rows, 0:W_C] * acc * _silu(pc[rows, 3 * W_C:4 * W_C])
        mix_ref[rows, W_A + W_B:D_MIX] = y_c.astype(BF16)

    mix = jnp.dot(mix_ref[...], wout_ref[...], preferred_element_type=F32)
    r2 = lax.rsqrt(jnp.mean(mix * mix, axis=-1, keepdims=True) + RMS_EPS)
    post = gate * gpost_ref[...]
    ys = [xs[t] + (mix[t * N:(t + 1) * N] * r2[t * N:(t + 1) * N]) * post for t in range(S)]
    for t in range(S):
        xs_ref[n, t] = ys[t]

    @pl.when(l == DEPTH - 1)
    def _():
        for t in range(S):
            y_ref[t] = ys[t]


def _sample_layers(x_t, mod_all, g_pre, g_post, w_in_b, w_out_b, a_conv_w, a_conv_b,
                   a_ln_g, a_ln_b, b_ln_g, b_ln_b, wss, bss, c_conv_w, ha_t, hc_t):
    S, N = DEC_SEQ, SAMPLE_N
    nsplit = DEC_BATCH // N
    row = lambda w: pl.BlockSpec((None, 1, w), lambda l, n: (l, 0, 0))
    lay3 = lambda a, b: pl.BlockSpec((None, a, b), lambda l, n: (l, 0, 0))
    lay4 = lambda a, b, c: pl.BlockSpec((None, a, b, c), lambda l, n: (l, 0, 0, 0))
    seq4 = lambda a, c: pl.BlockSpec((None, a, N, c), lambda l, n: (l, 0, n, 0))
    return pl.pallas_call(
        _sample_kernel,
        grid=(DEPTH, nsplit),
        in_specs=[
            pl.BlockSpec((S, N, D_MODEL), lambda l, n: (0, jnp.where(l == 0, n, 0), 0)),
            pl.BlockSpec((None, N, 3 * D_MODEL), lambda l, n: (l, n, 0)),
            row(D_MODEL), row(D_MODEL),
            *[pl.BlockSpec(memory_space=pl.ANY)] * (2 * DEPTH),
            lay3(CONV_A, W_A),
            row(W_A), row(W_A), row(W_A), row(W_B), row(W_B),
            lay4(S, S, W_B), lay3(S, W_B), lay3(CONV_C, W_C),
            seq4(CONV_A - 1, W_A), seq4(CONV_C - 1, W_C),
        ],
        out_specs=[
            pl.BlockSpec((S, N, D_MODEL), lambda l, n: (0, jnp.where(l == DEPTH - 1, n, 0), 0)),
            seq4(CONV_A - 1, W_A), seq4(CONV_C - 1, W_C), seq4(S, W_B),
        ],
        out_shape=[
            jax.ShapeDtypeStruct((S, DEC_BATCH, D_MODEL), F32),
            jax.ShapeDtypeStruct((DEPTH, CONV_A - 1, DEC_BATCH, W_A), F32),
            jax.ShapeDtypeStruct((DEPTH, CONV_C - 1, DEC_BATCH, W_C), F32),
            jax.ShapeDtypeStruct((DEPTH, S, DEC_BATCH, W_B), F32),
        ],
        scratch_shapes=[
            pltpu.VMEM((2, D_MODEL, P_IN), BF16),
            pltpu.VMEM((2, D_MIX, D_MODEL), BF16),
            pltpu.SemaphoreType.DMA((2, 2)),
            pltpu.VMEM((nsplit, S, N, D_MODEL), F32),
            pltpu.VMEM((S * N, D_MIX), BF16),
        ],
        compiler_params=pltpu.CompilerParams(
            dimension_semantics=("arbitrary", "arbitrary"), vmem_limit_bytes=VMEM_LIMIT),
        name="sample_layers",
    )(x_t, mod_all, g_pre, g_post, *w_in_b, *w_out_b, a_conv_w, a_conv_b,
      a_ln_g, a_ln_b, b_ln_g, b_ln_b, wss, bss, c_conv_w, ha_t, hc_t)


def kernel(x_prompt, x_sample, c_prompt, c_sample, state_conv_a, state_conv_c, w_mod, b_mod,
           g_pre, g_post, w_in, w_out, a_conv_w, a_conv_b, a_ln_g, a_ln_b, b_ln_g, b_ln_b,
           b_ws, b_bs, c_conv_w):
    S = DEC_SEQ
    r3 = lambda p: p.reshape(DEPTH, 1, p.shape[-1])
    g_pre3, g_post3 = r3(g_pre), r3(g_post)
    acb3, alng3, alnb3, blng3, blnb3 = r3(a_conv_b), r3(a_ln_g), r3(a_ln_b), r3(b_ln_g), r3(b_ln_b)

    c_all = jnp.concatenate([c_sample, c_prompt], axis=0)
    mod_all, w_in_b0, w_out_b0 = _mod_call(c_all, w_mod, b_mod, w_in, w_out)
    w_in_b, w_out_b = [w_in_b0], [w_out_b0]
    mod4 = mod_all.reshape(DEPTH, N_ALL, 3, D_MODEL)

    bsb = jnp.repeat(jnp.swapaxes(b_bs, 1, 2), HEAD_W, axis=2)
    wss = jnp.repeat(jnp.transpose(b_ws[:, :, :S, :S], (0, 2, 3, 1)), HEAD_W, axis=3)
    bss = bsb[:, :S, :]

    xp = x_prompt.reshape(BATCH * SEQ, D_MODEL)
    ca_p, cc_p = [], []
    for l in range(DEPTH):
        outs = _prompt_layer(l, xp, mod4, g_pre3, g_post3, w_in_b[l], w_out_b[l], w_in, w_out, a_conv_w, acb3,
                             alng3, alnb3, blng3, blnb3, b_ws, bsb, c_conv_w)
        xp, ca, cc = outs[:3]
        ca_p.append(ca)
        cc_p.append(cc)
        if l + 1 < DEPTH:
            w_in_b.append(outs[3])
            w_out_b.append(outs[4])

    x_t = jnp.swapaxes(x_sample, 0, 1)
    ha_t = jnp.swapaxes(state_conv_a, 1, 2)
    hc_t = jnp.swapaxes(state_conv_c, 1, 2)
    y_t, ca_t, cc_t, v_t = _sample_layers(
        x_t, mod_all, g_pre3, g_post3, w_in_b, w_out_b, a_conv_w, acb3,
        alng3, alnb3, blng3, blnb3, wss, bss, c_conv_w, ha_t, hc_t)

    return (xp.reshape(BATCH, SEQ, D_MODEL),
            jnp.swapaxes(y_t, 0, 1),
            jnp.stack(ca_p), jnp.stack(cc_p),
            jnp.swapaxes(ca_t, 1, 2), jnp.swapaxes(cc_t, 1, 2), jnp.swapaxes(v_t, 1, 2))
```
